```python
import math
import jax, jax.numpy as jnp
from jax import lax
import numpy as np

D_MODEL = 2048
BATCH = 16
SEQ = 2048
DEPTH = 1
DEC_BATCH = 8
DEC_SEQ = 4096
PAST_LEN = 128

MEM_LEN = 256
GDN_HEADS = 8
GDN_HEAD_DIM = 128
GDN_WIDTH = GDN_HEADS * GDN_HEAD_DIM
GDN_CONV = 5
GDN_CHUNK = 64
DIFF_HEADS = 4
DIFF_HEAD_DIM = 128
DIFF_QK_WIDTH = DIFF_HEADS * 2 * DIFF_HEAD_DIM
DIFF_V_WIDTH = DIFF_HEADS * 2 * DIFF_HEAD_DIM
CROSS_HEADS = 4
CROSS_HEAD_DIM = 256
CROSS_WIDTH = CROSS_HEADS * CROSS_HEAD_DIM
N_BRANCH = 3
D_FF = 5504
FFN_CONV = 3
NUM_BUCKETS = 32
MAX_DISTANCE = 128
Q_BLOCK = 128
LN_EPS = 1e-5
RMS_EPS = 1e-6
L2_EPS = 1e-6
DEEPNORM_ALPHA = (2 * DEPTH) ** 0.25
DEEPNORM_BETA = (8 * DEPTH) ** -0.25
IN_SIZES = (GDN_WIDTH, GDN_WIDTH, GDN_WIDTH, GDN_WIDTH, 2 * GDN_HEADS, 2 * GDN_HEADS,
            DIFF_QK_WIDTH, DIFF_QK_WIDTH, DIFF_V_WIDTH, CROSS_WIDTH)
IN_COLS = sum(IN_SIZES)
IN_OFFSETS = tuple(int(o) for o in np.cumsum(IN_SIZES)[:-1])

kernel_name = 'hybrid_bidir_gdn_diffattn_encoder'


def layer_norm(x, g, b):
    xf = x.astype(jnp.float32)
    mu = jnp.mean(xf, -1, keepdims=True)
    xc = xf - mu
    var = jnp.mean(xc * xc, -1, keepdims=True)
    return (xc * lax.rsqrt(var + LN_EPS) * g.astype(jnp.float32) + b.astype(jnp.float32)).astype(x.dtype)


def rms_norm(x, w):
    return x * lax.rsqrt(jnp.mean(x * x, -1, keepdims=True) + RMS_EPS) * w.astype(jnp.float32)


def l2_normalize(x):
    return x * lax.rsqrt(jnp.sum(x * x, -1, keepdims=True) + L2_EPS)


def depthwise_conv(x, w):
    ksz, ch = w.shape
    return lax.conv_general_dilated(x, w[:, None, :].astype(x.dtype), window_strides=(1,),
                                    padding=[(ksz // 2, ksz // 2)],
                                    dimension_numbers=('NWC', 'WIO', 'NWC'),
                                    feature_group_count=ch)


def t5_bucket(rel):
    nb = NUM_BUCKETS // 2
    max_exact = nb // 2
    ret = jnp.where(rel > 0, nb, 0)
    n = jnp.abs(rel)
    nf = jnp.maximum(n, 1).astype(jnp.float32)
    large = max_exact + (jnp.log(nf / max_exact) / math.log(MAX_DISTANCE / max_exact)
                         * (nb - max_exact)).astype(jnp.int32)
    large = jnp.minimum(large, nb - 1)
    return ret + jnp.where(n < max_exact, n, large)


def gated_delta_chunked(q, k, v, g, beta):
    bsz, nh, seq, dk = q.shape
    dv = v.shape[-1]
    nc = seq // GDN_CHUNK
    q = q.reshape(bsz, nh, nc, GDN_CHUNK, dk)
    k = k.reshape(bsz, nh, nc, GDN_CHUNK, dk)
    v = v.reshape(bsz, nh, nc, GDN_CHUNK, dv)
    g = g.reshape(bsz, nh, nc, GDN_CHUNK)
    beta = beta.reshape(bsz, nh, nc, GDN_CHUNK)
    gc = jnp.cumsum(g, -1)
    idx = jnp.arange(GDN_CHUNK)
    incl = idx[:, None] >= idx[None, :]
    strict = idx[:, None] > idx[None, :]
    decay = jnp.exp(jnp.where(incl, gc[..., :, None] - gc[..., None, :], -jnp.inf))
    lmat = jnp.where(strict, beta[..., :, None] * jnp.einsum('bhncd,bhnjd->bhncj', k, k) * decay, 0.0)
    rhs = jnp.concatenate([beta[..., None] * v, (beta * jnp.exp(gc))[..., None] * k], -1)
    sol = lax.linalg.triangular_solve(lmat, rhs, left_side=True, lower=True, unit_diagonal=True)
    wv, wk = sol[..., :dv], sol[..., dv:]
    qk = jnp.einsum('bhncd,bhnjd->bhncj', q, k) * decay
    q_dec = q * jnp.exp(gc)[..., None]
    g_last = gc[..., -1]
    k_dec = k * jnp.exp(g_last[..., None] - gc)[..., None]

    def chunk_step(state, xs):
        wv_c, wk_c, qk_c, qd_c, kd_c, gl_c = xs
        u = wv_c - jnp.einsum('bhck,bhkv->bhcv', wk_c, state)
        o = jnp.einsum('bhck,bhkv->bhcv', qd_c, state) + jnp.einsum('bhcj,bhjv->bhcv', qk_c, u)
        state = jnp.exp(gl_c)[..., None, None] * state + jnp.einsum('bhck,bhcv->bhkv', kd_c, u)
        return state, o

    xs = tuple(jnp.moveaxis(t, 2, 0) for t in (wv, wk, qk, q_dec, k_dec, g_last))
    state0 = jnp.zeros((bsz, nh, dk, dv), jnp.float32)
    _, o = lax.scan(chunk_step, state0, xs)
    return jnp.moveaxis(o, 0, 2).reshape(bsz, nh, seq, dv)


def gdn_branch(q, k, v, z, a, b, conv_w, a_log, dt_bias, norm_w):
    bsz, seq, _ = q.shape
    dtype = q.dtype
    qkv = jax.nn.silu(depthwise_conv(jnp.concatenate([q, k, v], -1), conv_w).astype(jnp.float32))
    qkv = qkv.reshape(bsz, seq, 3, GDN_HEADS, GDN_HEAD_DIM).transpose(2, 0, 3, 1, 4)
    qh = l2_normalize(qkv[0]) * GDN_HEAD_DIM ** -0.5
    kh = l2_normalize(qkv[1])
    vh = qkv[2]
    a = a.astype(jnp.float32).reshape(bsz, seq, 2, GDN_HEADS).transpose(2, 0, 3, 1)
    b = b.astype(jnp.float32).reshape(bsz, seq, 2, GDN_HEADS).transpose(2, 0, 3, 1)
    a_log = a_log.astype(jnp.float32)[:, None, :, None]
    dt_bias = dt_bias.astype(jnp.float32)[:, None, :, None]
    g = -jnp.exp(a_log) * jax.nn.softplus(a + dt_bias)
    beta = jax.nn.sigmoid(b)
    o_fwd = gated_delta_chunked(qh, kh, vh, g[0], beta[0])
    flip = lambda t: jnp.flip(t, axis=2)
    o_bwd = flip(gated_delta_chunked(flip(qh), flip(kh), flip(vh), flip(g[1]), flip(beta[1])))
    o = rms_norm(o_fwd + o_bwd, norm_w).transpose(0, 2, 1, 3)
    o = o * jax.nn.silu(z.astype(jnp.float32).reshape(bsz, seq, GDN_HEADS, GDN_HEAD_DIM))
    return o.reshape(bsz, seq, GDN_WIDTH).astype(dtype)


def diff_attention_branch(q, k, v, lam_params, norm_w, rel_bias, lambda_init):
    bsz, seq, _ = q.shape
    nb = seq // Q_BLOCK
    qb = (q * DIFF_HEAD_DIM ** -0.5).reshape(bsz, nb, Q_BLOCK, DIFF_HEADS, 2, DIFF_HEAD_DIM)
    qb = qb.transpose(1, 0, 3, 4, 2, 5)
    kh = k.reshape(bsz, seq, DIFF_HEADS, 2, DIFF_HEAD_DIM).transpose(0, 2, 3, 1, 4)
    vh = v.reshape(bsz, seq, DIFF_HEADS, 2 * DIFF_HEAD_DIM).transpose(0, 2, 1, 3)
    lp = lam_params.astype(jnp.float32)
    lam = jnp.exp(jnp.sum(lp[0] * lp[1])) - jnp.exp(jnp.sum(lp[2] * lp[3])) + lambda_init
    kpos = jnp.arange(seq)

    def attend_block(args):
        q_blk, bi = args
        qpos = bi * Q_BLOCK + jnp.arange(Q_BLOCK)
        bias = rel_bias[t5_bucket(kpos[None, :] - qpos[:, None])]
        bias = jnp.transpose(bias, (2, 0, 1)).astype(jnp.float32)
        s = jnp.einsum('bhmqd,bhmkd->bhmqk', q_blk, kh,
                       preferred_element_type=jnp.float32) + bias[None, :, None]
        p = jax.nn.softmax(s, axis=-1)
        w = (p[:, :, 0] - lam * p[:, :, 1]).astype(vh.dtype)
        return jnp.einsum('bhqk,bhkd->bhqd', w, vh)

    o = lax.map(attend_block, (qb, jnp.arange(nb)))
    o = rms_norm(o.astype(jnp.float32), norm_w) * (1.0 - lambda_init)
    return o.transpose(1, 0, 3, 2, 4).reshape(bsz, seq, DIFF_V_WIDTH).astype(q.dtype)


def memory_cross_attention(q, mem, w_mem_kv):
    bsz, seq, _ = q.shape
    kv = (mem @ w_mem_kv).reshape(bsz, mem.shape[1], 2, CROSS_HEADS, CROSS_HEAD_DIM)
    qh = (q * CROSS_HEAD_DIM ** -0.5).reshape(bsz, seq, CROSS_HEADS, CROSS_HEAD_DIM)
    s = jnp.einsum('bqhd,bkhd->bhqk', qh, kv[:, :, 0], preferred_element_type=jnp.float32)
    p = jax.nn.softmax(s, axis=-1).astype(q.dtype)
    o = jnp.einsum('bhqk,bkhd->bqhd', p, kv[:, :, 1])
    return o.reshape(bsz, seq, CROSS_WIDTH)


def encoder_trunk(x, mem, rel_bias, w_in, gdn_conv, gdn_a_log, gdn_dt_bias, gdn_norm_w,
                  diff_lambda, diff_norm_w, w_mem_kv, w_gate, b_gate, w_branch_gdn,
                  w_branch_diff, w_branch_cross, w_out, ln1_g, ln1_b, w_up, ffn_conv,
                  w_down, ln2_g, ln2_b):
    bsz, seq, _ = x.shape
    for l in range(DEPTH):
        lambda_init = 0.8 - 0.6 * math.exp(-0.3 * l)
        proj = x @ w_in[l]
        gq, gk, gv, gz, ga, gb, dq, dk, dv, cq = jnp.split(proj, IN_OFFSETS, axis=-1)
        y_gdn = gdn_branch(gq, gk, gv, gz, ga, gb, gdn_conv[l], gdn_a_log[l], gdn_dt_bias[l], gdn_norm_w[l])
        y_diff = diff_attention_branch(dq, dk, dv, diff_lambda[l], diff_norm_w[l], rel_bias, lambda_init)
        y_cross = memory_cross_attention(cq, mem, w_mem_kv[l])
        gates = jax.nn.sigmoid(x @ w_gate[l] + b_gate[l]).reshape(bsz, seq, N_BRANCH, D_MODEL)
        merged = (gates[:, :, 0] * (y_gdn @ w_branch_gdn[l])
                  + gates[:, :, 1] * (y_diff @ w_branch_diff[l])
                  + gates[:, :, 2] * (y_cross @ w_branch_cross[l]))
        x = layer_norm(DEEPNORM_ALPHA * x + merged @ w_out[l], ln1_g[l], ln1_b[l])
        up = depthwise_conv(x @ w_up[l], ffn_conv[l])
        u_gate, u_val = jnp.split(up, 2, axis=-1)
        x = layer_norm(DEEPNORM_ALPHA * x + (jax.nn.silu(u_gate) * u_val) @ w_down[l], ln2_g[l], ln2_b[l])
    return x


def setup_inputs(seed: int = 0) -> dict:
    key = jax.random.key(seed)
    ks = jax.random.split(key, 32)
    f32 = jnp.float32

    def nrm(k, shape, scale):
        return jax.random.normal(k, shape, f32) * scale

    dt = jnp.exp(jax.random.uniform(ks[9], (DEPTH, 2, GDN_HEADS), f32, math.log(1e-3), math.log(1e-1)))
    return {
        'x_prompt': nrm(ks[0], (BATCH, SEQ, D_MODEL), 1.0),
        'x_sample': nrm(ks[1], (DEC_BATCH, DEC_SEQ, D_MODEL), 1.0),
        'mem_prompt': nrm(ks[2], (BATCH, MEM_LEN, D_MODEL), 1.0),
        'mem_sample': nrm(ks[3], (DEC_BATCH, MEM_LEN, D_MODEL), 1.0),
        'rel_bias': nrm(ks[4], (NUM_BUCKETS, DIFF_HEADS), 0.2),
        'w_in': nrm(ks[5], (DEPTH, D_MODEL, IN_COLS), D_MODEL ** -0.5),
        'gdn_conv': nrm(ks[6], (DEPTH, GDN_CONV, 3 * GDN_WIDTH), GDN_CONV ** -0.5),
        'gdn_a_log': jnp.log(jax.random.uniform(ks[7], (DEPTH, 2, GDN_HEADS), f32, 1.0, 16.0)),
        'gdn_dt_bias': dt + jnp.log(-jnp.expm1(-dt)),
        'gdn_norm_w': 1.0 + nrm(ks[8], (DEPTH, GDN_HEAD_DIM), 0.02),
        'diff_lambda': nrm(ks[10], (DEPTH, 4, DIFF_HEAD_DIM), 0.1),
        'diff_norm_w': 1.0 + nrm(ks[11], (DEPTH, 2 * DIFF_HEAD_DIM), 0.02),
        'w_mem_kv': nrm(ks[12], (DEPTH, D_MODEL, 2 * CROSS_WIDTH), D_MODEL ** -0.5),
        'w_gate': nrm(ks[13], (DEPTH, D_MODEL, N_BRANCH * D_MODEL), D_MODEL ** -0.5),
        'b_gate': nrm(ks[14], (DEPTH, N_BRANCH * D_MODEL), 0.1),
        'w_branch_gdn': nrm(ks[15], (DEPTH, GDN_WIDTH, D_MODEL), GDN_WIDTH ** -0.5 * DEEPNORM_BETA),
        'w_branch_diff': nrm(ks[16], (DEPTH, DIFF_V_WIDTH, D_MODEL), DIFF_V_WIDTH ** -0.5 * DEEPNORM_BETA),
        'w_branch_cross': nrm(ks[17], (DEPTH, CROSS_WIDTH, D_MODEL), CROSS_WIDTH ** -0.5 * DEEPNORM_BETA),
        'w_out': nrm(ks[18], (DEPTH, D_MODEL, D_MODEL), D_MODEL ** -0.5 * DEEPNORM_BETA),
        'ln1_g': 1.0 + nrm(ks[19], (DEPTH, D_MODEL), 0.02),
        'ln1_b': nrm(ks[20], (DEPTH, D_MODEL), 0.02),
        'w_up': nrm(ks[21], (DEPTH, D_MODEL, 2 * D_FF), D_MODEL ** -0.5),
        'ffn_conv': nrm(ks[22], (DEPTH, FFN_CONV, 2 * D_FF), FFN_CONV ** -0.5),
        'w_down': nrm(ks[23], (DEPTH, D_FF, D_MODEL), D_FF ** -0.5 * DEEPNORM_BETA),
        'ln2_g': 1.0 + nrm(ks[24], (DEPTH, D_MODEL), 0.02),
        'ln2_b': nrm(ks[25], (DEPTH, D_MODEL), 0.02),
    }


def reference(x_prompt, x_sample, mem_prompt, mem_sample, rel_bias, w_in, gdn_conv, gdn_a_log,
              gdn_dt_bias, gdn_norm_w, diff_lambda, diff_norm_w, w_mem_kv, w_gate, b_gate,
              w_branch_gdn, w_branch_diff, w_branch_cross, w_out, ln1_g, ln1_b, w_up, ffn_conv,
              w_down, ln2_g, ln2_b):
    y_prompt = encoder_trunk(x_prompt, mem_prompt, rel_bias, w_in, gdn_conv, gdn_a_log, gdn_dt_bias,
                             gdn_norm_w, diff_lambda, diff_norm_w, w_mem_kv, w_gate, b_gate,
                             w_branch_gdn, w_branch_diff, w_branch_cross, w_out, ln1_g, ln1_b,
                             w_up, ffn_conv, w_down, ln2_g, ln2_b)
    y_sample = encoder_trunk(x_sample, mem_sample, rel_bias, w_in, gdn_conv, gdn_a_log, gdn_dt_bias,
                             gdn_norm_w, diff_lambda, diff_norm_w, w_mem_kv, w_gate, b_gate,
                             w_branch_gdn, w_branch_diff, w_branch_cross, w_out, ln1_g, ln1_b,
                             w_up, ffn_conv, w_down, ln2_g, ln2_b)
    return (y_prompt, y_sample)
```

```python
import functools
import math

import jax
import jax.numpy as jnp
import numpy as np
from jax import lax
from jax.experimental import pallas as pl
from jax.experimental.pallas import tpu as pltpu

f32 = jnp.float32
bf16 = jnp.bfloat16

D_MODEL = 2048
GDN_HEADS = 8
GDN_HEAD_DIM = 128
GDN_WIDTH = 1024
GDN_CONV = 5
GDN_CHUNK = 64
DIFF_HEADS = 4
DIFF_HEAD_DIM = 128
CROSS_HEADS = 4
CROSS_HEAD_DIM = 256
N_BRANCH = 3
D_FF = 5504
NUM_BUCKETS = 32
MAX_DISTANCE = 128
LN_EPS = 1e-5
RMS_EPS = 1e-6
L2_EPS = 1e-6
DEPTH = 1
DEEPNORM_ALPHA = (2 * DEPTH) ** 0.25
LAMBDA_INIT = 0.8 - 0.6 * math.exp(-0.3 * 0)

LANES = 128
MXU_WIDTH = 256
VMEM_LIMIT_BYTES = 56 * 1024 * 1024

PROJ_LIN = 8192
COL_GQ, COL_GK, COL_GV, COL_GZ = 0, 1024, 2048, 3072
COL_DQ, COL_DK, COL_DV, COL_CQ = 4096, 5120, 6144, 7168
COL_GATE = PROJ_LIN
PROJ_COLS = PROJ_LIN + N_BRANCH * D_MODEL
D_FF_PAD = 5632

NT_DIMS = (((1,), (1,)), ((), ()))


def _cparams(sem, vmem=VMEM_LIMIT_BYTES):
    return pltpu.CompilerParams(dimension_semantics=sem, vmem_limit_bytes=vmem)


def _sigmoid(x):
    return 1.0 / (1.0 + jnp.exp(-x))


def _silu(x):
    return x * _sigmoid(x)


def _layer_norm(y, g, b):
    mu = jnp.mean(y, axis=-1, keepdims=True)
    yc = y - mu
    var = jnp.mean(yc * yc, axis=-1, keepdims=True)
    return yc * lax.rsqrt(var + LN_EPS) * g + b


def _inproj_kernel(x_ref, w_ref, b_ref, wab_ref, o_ref, ab_ref, xb_ref, *, n_lin_tiles):
    j = pl.program_id(1)

    @pl.when(j == 0)
    def _():
        xb = x_ref[...].astype(bf16)
        xb_ref[...] = xb
        ab_ref[...] = jnp.dot(xb, wab_ref[...], preferred_element_type=f32)

    acc = jnp.dot(xb_ref[...], w_ref[...], preferred_element_type=f32)

    @pl.when(j < n_lin_tiles)
    def _():
        o_ref[...] = acc.astype(bf16)

    @pl.when(j >= n_lin_tiles)
    def _():
        o_ref[...] = _sigmoid(acc + b_ref[...]).astype(bf16)


def _inproj(x2, w_all, b_all, w_ab, *, tm, tn):
    T = x2.shape[0]
    n_lin_tiles = PROJ_LIN // tn
    return pl.pallas_call(
        functools.partial(_inproj_kernel, n_lin_tiles=n_lin_tiles),
        out_shape=(jax.ShapeDtypeStruct((T, PROJ_COLS), bf16),
                   jax.ShapeDtypeStruct((T, LANES), f32)),
        grid=(T // tm, PROJ_COLS // tn),
        in_specs=[pl.BlockSpec((tm, D_MODEL), lambda i, j: (i, 0)),
                  pl.BlockSpec((D_MODEL, tn), lambda i, j: (0, j)),
                  pl.BlockSpec((1, tn), lambda i, j: (0, j)),
                  pl.BlockSpec((D_MODEL, LANES), lambda i, j: (0, 0))],
        out_specs=(pl.BlockSpec((tm, tn), lambda i, j: (i, j)),
                   pl.BlockSpec((tm, LANES), lambda i, j: (i, 0))),
        scratch_shapes=[pltpu.VMEM((tm, D_MODEL), bf16)],
        compiler_params=_cparams(("parallel", "arbitrary")),
        name="inproj",
    )(x2, w_all, b_all, w_ab)


def _mm_kernel(x_ref, w_ref, o_ref):
    o_ref[...] = jnp.dot(x_ref[...].astype(bf16), w_ref[...], preferred_element_type=f32).astype(o_ref.dtype)


def _matmul(x2, w, *, tm, out_dtype=bf16):
    T, K = x2.shape
    N = w.shape[1]
    return pl.pallas_call(
        _mm_kernel,
        out_shape=jax.ShapeDtypeStruct((T, N), out_dtype),
        grid=(T // tm,),
        in_specs=[pl.BlockSpec((tm, K), lambda i: (i, 0)),
                  pl.BlockSpec((K, N), lambda i: (0, 0), pipeline_mode=pl.Buffered(1))],
        out_specs=pl.BlockSpec((tm, N), lambda i: (i, 0)),
        compiler_params=_cparams(("parallel",)),
        name="mem_kv",
    )(x2, w)


SB = 2 * GDN_CHUNK
CONV_ROWS = 256
CONV_PAD = 8


def _split_bf16(x):
    hi = x.astype(bf16)
    lo = (x - hi.astype(f32)).astype(bf16)
    return hi, lo


def _gdn_kernel(alog_ref, dtb_ref, q_ref, k_ref, v_ref, z_ref, ab_ref, cq_ref, ck_ref, cv_ref,
                nw_ref, o_ref, qn_ref, kn_ref, vn_ref, xp_ref, wv_ref, wk_ref, qkd_ref, qd_ref,
                kdt_ref, egl_ref, of_ref, ob_ref, *, S):
    h = pl.program_id(1)
    nsb = S // SB
    hd = GDN_HEAD_DIM

    zpad = jnp.zeros((CONV_PAD, hd), f32)
    xp_ref[0:CONV_PAD, :] = zpad
    xp_ref[S + CONV_PAD:S + 2 * CONV_PAD, :] = zpad

    def conv_into(src_ref, w_ref, dst_ref, mode):
        xp_ref[CONV_PAD:S + CONV_PAD, :] = src_ref[0].astype(f32)
        w = w_ref[...]

        def body(r, c):
            r0 = pl.multiple_of(r * CONV_ROWS, CONV_ROWS)
            acc = jnp.zeros((CONV_ROWS, hd), f32)
            for j in range(GDN_CONV):
                acc = acc + xp_ref[pl.ds(r0 + CONV_PAD - GDN_CONV // 2 + j, CONV_ROWS), :] * w[j:j + 1, :]
            y = _silu(acc)
            if mode != "v":
                y = y * lax.rsqrt(jnp.sum(y * y, axis=-1, keepdims=True) + L2_EPS)
            if mode == "q":
                y = y * (hd ** -0.5)
            dst_ref[pl.ds(r0, CONV_ROWS), :] = y
            return c

        lax.fori_loop(0, S // CONV_ROWS, body, 0)

    conv_into(q_ref, cq_ref, qn_ref, "q")
    conv_into(k_ref, ck_ref, kn_ref, "k")
    conv_into(v_ref, cv_ref, vn_ref, "v")

    ri = lax.broadcasted_iota(jnp.int32, (SB, SB), 0)
    ci = lax.broadcasted_iota(jnp.int32, (SB, SB), 1)
    same = (ri // GDN_CHUNK) == (ci // GDN_CHUNK)
    incl = (same & (ri >= ci), same & (ri <= ci))
    strict = (same & (ri > ci), same & (ri < ci))
    r2 = lax.broadcasted_iota(jnp.int32, (2 * SB, 2 * SB), 0) % SB
    c2 = lax.broadcasted_iota(jnp.int32, (2 * SB, 2 * SB), 1) // SB
    sel_a = (r2 == (c2 * GDN_HEADS + h)).astype(bf16)
    sel_b = (r2 == (2 * GDN_HEADS + c2 * GDN_HEADS + h)).astype(bf16)

    def phase_a(sb, c):
        r0 = pl.multiple_of(sb * SB, SB)
        rows = pl.ds(r0, SB)
        kn = kn_ref[rows, :]
        qn = qn_ref[rows, :]
        vn = vn_ref[rows, :]
        knb = kn.astype(bf16)
        kk = lax.dot_general(knb, knb, NT_DIMS, preferred_element_type=f32)
        qk = lax.dot_general(qn.astype(bf16), knb, NT_DIMS, preferred_element_type=f32)
        ab_hi, ab_lo = _split_bf16(ab_ref[0, rows, :])
        ab2 = jnp.concatenate([ab_hi, ab_lo], axis=1)
        a_both = jnp.dot(ab2, sel_a, preferred_element_type=f32)
        b_both = jnp.dot(ab2, sel_b, preferred_element_type=f32)
        for d in range(2):
            a_b = a_both[:, d * hd:(d + 1) * hd]
            b_b = b_both[:, d * hd:(d + 1) * hd]
            xs = a_b + dtb_ref[d, h]
            softplus = jnp.maximum(xs, 0.0) + jnp.log1p(jnp.exp(-jnp.abs(xs)))
            g = -jnp.exp(jnp.full((1, hd), alog_ref[d, h], f32)) * softplus
            beta = _sigmoid(b_b)
            g_hi, g_lo = _split_bf16(g)
            tri = incl[d].astype(bf16)
            gc = jnp.dot(jnp.concatenate([tri, tri], axis=1), jnp.concatenate([g_hi, g_lo], axis=0),
                         preferred_element_type=f32)
            diff = gc - gc.T
            decay = jnp.where(incl[d], jnp.exp(jnp.where(incl[d], diff, 0.0)), 0.0)
            lmat = jnp.where(strict[d], beta * kk * decay, 0.0).astype(bf16)
            egc = jnp.exp(gc)
            y = jnp.concatenate([beta * vn, (beta * egc) * kn], axis=1)
            y = y - jnp.dot(lmat, y.astype(bf16), preferred_element_type=f32)
            p = jnp.dot(lmat, lmat, preferred_element_type=f32)
            for it in range(5):
                pb = p.astype(bf16)
                y = y + jnp.dot(pb, y.astype(bf16), preferred_element_type=f32)
                if it < 4:
                    p = jnp.dot(pb, pb, preferred_element_type=f32)
            last = (GDN_CHUNK - 1) if d == 0 else 0
            gl = jnp.concatenate(
                [jnp.broadcast_to(gc[last:last + 1, :], (GDN_CHUNK, hd)),
                 jnp.broadcast_to(gc[GDN_CHUNK + last:GDN_CHUNK + last + 1, :], (GDN_CHUNK, hd))], axis=0)
            wv_ref[d, rows, :] = y[:, :hd]
            wk_ref[d, rows, :] = y[:, hd:].astype(bf16)
            qkd_ref[d, rows, :] = jnp.where(incl[d], qk * decay, 0.0).astype(bf16)
            qd_ref[d, rows, :] = (qn * egc).astype(bf16)
            kdt_ref[d, rows, :] = (kn * jnp.exp(gl - gc)).T.astype(bf16)
            egl = jnp.exp(gl)
            e0 = pl.multiple_of(sb * 16, 16)
            egl_ref[d, pl.ds(e0, 8), :] = egl[0:8, :]
            egl_ref[d, pl.ds(e0 + 8, 8), :] = egl[GDN_CHUNK:GDN_CHUNK + 8, :]
        return c

    lax.fori_loop(0, nsb, phase_a, 0)

    zero_half = jnp.zeros((GDN_CHUNK, hd), bf16)

    def chunk_step(d, sbi, half, state, out_ref):
        r0 = pl.multiple_of(sbi * SB, SB)
        rows = pl.ds(r0 + half * GDN_CHUNK, GDN_CHUNK)
        sbf = state.astype(bf16)
        u = wv_ref[d, rows, :] - jnp.dot(wk_ref[d, rows, :], sbf, preferred_element_type=f32)
        ub = u.astype(bf16)
        upad = jnp.concatenate([ub, zero_half] if half == 0 else [zero_half, ub], axis=0)
        out_ref[rows, :] = (jnp.dot(qd_ref[d, rows, :], sbf, preferred_element_type=f32)
                            + jnp.dot(qkd_ref[d, rows, :], upad, preferred_element_type=f32))
        egl = egl_ref[d, pl.ds(pl.multiple_of(sbi * 16, 16) + 8 * half, 8), :][0:1, :]
        return egl * state + jnp.dot(kdt_ref[d, pl.ds(r0, SB), :], upad, preferred_element_type=f32)

    def phase_b(t, carry):
        sf, sbk = carry
        sf = chunk_step(0, t, 0, sf, of_ref)
        sbk = chunk_step(1, nsb - 1 - t, 1, sbk, ob_ref)
        sf = chunk_step(0, t, 1, sf, of_ref)
        sbk = chunk_step(1, nsb - 1 - t, 0, sbk, ob_ref)
        return sf, sbk

    z0 = jnp.zeros((hd, hd), f32)
    lax.fori_loop(0, nsb, phase_b, (z0, z0))

    nw = nw_ref[...]

    def fin(r, c):
        rows = pl.ds(pl.multiple_of(r * CONV_ROWS, CONV_ROWS), CONV_ROWS)
        o = of_ref[rows, :] + ob_ref[rows, :]
        o = o * lax.rsqrt(jnp.mean(o * o, axis=-1, keepdims=True) + RMS_EPS) * nw
        o_ref[0, rows, :] = (o * _silu(z_ref[0, rows, :].astype(f32))).astype(bf16)
        return c

    lax.fori_loop(0, S // CONV_ROWS, fin, 0)


def _gdn(proj3, ab3, conv_w, a_log, dt_bias, norm_w):
    B, S, _ = proj3.shape
    hb = GDN_HEAD_DIM
    nsb = S // SB

    def col(base):
        return pl.BlockSpec((1, S, hb), lambda b, h, base=base: (b, 0, base // hb + h))

    def cw(base):
        return pl.BlockSpec((GDN_CONV, hb), lambda b, h, base=base: (0, base // hb + h))

    smem = pl.BlockSpec(memory_space=pltpu.SMEM)
    return pl.pallas_call(
        functools.partial(_gdn_kernel, S=S),
        out_shape=jax.ShapeDtypeStruct((B, S, GDN_WIDTH), bf16),
        grid=(B, GDN_HEADS),
        in_specs=[smem, smem, col(COL_GQ), col(COL_GK), col(COL_GV), col(COL_GZ),
                  pl.BlockSpec((1, S, LANES), lambda b, h: (b, 0, 0)),
                  cw(0), cw(GDN_WIDTH), cw(2 * GDN_WIDTH),
                  pl.BlockSpec((1, hb), lambda b, h: (0, 0))],
        out_specs=pl.BlockSpec((1, S, hb), lambda b, h: (b, 0, h)),
        scratch_shapes=[pltpu.VMEM((S, hb), f32), pltpu.VMEM((S, hb), f32), pltpu.VMEM((S, hb), f32),
                        pltpu.VMEM((S + 2 * CONV_PAD, hb), f32),
                        pltpu.VMEM((2, S, hb), f32),
                        pltpu.VMEM((2, S, hb), bf16), pltpu.VMEM((2, S, hb), bf16),
                        pltpu.VMEM((2, S, hb), bf16), pltpu.VMEM((2, S, hb), bf16),
                        pltpu.VMEM((2, nsb * 16, hb), f32),
                        pltpu.VMEM((S, hb), f32), pltpu.VMEM((S, hb), f32)],
        compiler_params=_cparams(("parallel", "arbitrary")),
        name="gdn",
    )(a_log, dt_bias, proj3, proj3, proj3, proj3, ab3, conv_w, conv_w, conv_w, norm_w)


N_BIAS_TILES = 6
KV_TILE = 2 * LANES


def _bias_kernel(rb_ref, bucket_ref, lp_ref, bias_ref, lam_ref):
    bucket = bucket_ref[...]
    for hh in range(DIFF_HEADS):
        acc = jnp.zeros(bucket.shape, f32)
        for bk in range(NUM_BUCKETS):
            acc = jnp.where(bucket == bk, rb_ref[bk, hh], acc)
        bias_ref[hh] = acc
    lp = lp_ref[...]
    lam = (jnp.exp(jnp.sum(lp[0:1] * lp[1:2], axis=-1, keepdims=True))
           - jnp.exp(jnp.sum(lp[2:3] * lp[3:4], axis=-1, keepdims=True)) + LAMBDA_INIT)
    lam_ref[...] = jnp.broadcast_to(lam, lam_ref.shape)


def _t5_bucket(rel):
    nb = NUM_BUCKETS // 2
    max_exact = nb // 2
    ret = jnp.where(rel > 0, nb, 0)
    n = jnp.abs(rel)
    nf = jnp.maximum(n, 1).astype(jnp.float32)
    large = max_exact + (jnp.log(nf / max_exact) / math.log(MAX_DISTANCE / max_exact)
                         * (nb - max_exact)).astype(jnp.int32)
    large = jnp.minimum(large, nb - 1)
    return ret + jnp.where(n < max_exact, n, large)


def _bias_tiles(rel_bias, lam_params):
    delta = jnp.arange(N_BIAS_TILES, dtype=jnp.int32)[:, None, None] - 3
    rel = (LANES * delta + jnp.arange(KV_TILE, dtype=jnp.int32)[None, None, :]
           - jnp.arange(LANES, dtype=jnp.int32)[None, :, None])
    bucket = _t5_bucket(rel).astype(jnp.int32)
    return pl.pallas_call(
        _bias_kernel,
        out_shape=(jax.ShapeDtypeStruct((DIFF_HEADS, N_BIAS_TILES, LANES, KV_TILE), f32),
                   jax.ShapeDtypeStruct((8, LANES), f32)),
        in_specs=[pl.BlockSpec(memory_space=pltpu.SMEM),
                  pl.BlockSpec(memory_space=pltpu.VMEM),
                  pl.BlockSpec(memory_space=pltpu.VMEM)],
        out_specs=(pl.BlockSpec(memory_space=pltpu.VMEM), pl.BlockSpec(memory_space=pltpu.VMEM)),
        name="rel_bias",
    )(rel_bias, bucket, lam_params)


def _diff_kernel(q_ref, k_ref, v_ref, bias_ref, lam_ref, nw_ref, o_ref, s_ref, e_ref, *, S, QB):
    qi = pl.program_id(2)
    n_tiles = S // KV_TILE
    nsub = QB // LANES
    dh = DIFF_HEAD_DIM
    scale = dh ** -0.5
    q = q_ref[0]
    outs = []
    for m in range(2):
        qm = q[:, m * dh:(m + 1) * dh]

        def scores(jt, mx, m=m, qm=qm):
            off = pl.multiple_of(jt * KV_TILE, KV_TILE)
            kj = k_ref[0, pl.ds(off, KV_TILE), m * dh:(m + 1) * dh]
            s = lax.dot_general(qm, kj, NT_DIMS, preferred_element_type=f32) * scale
            parts = []
            for r in range(nsub):
                idx = jnp.clip(2 * jt - (qi * nsub + r), -3, 2) + 3
                parts.append(s[r * LANES:(r + 1) * LANES, :] + bias_ref[0, idx])
            s = jnp.concatenate(parts, axis=0) if nsub > 1 else parts[0]
            s_ref[:, pl.ds(off, KV_TILE)] = s
            return jnp.maximum(mx, jnp.maximum(s[:, :LANES], s[:, LANES:]))

        mx = lax.fori_loop(0, n_tiles, scores, jnp.full((QB, LANES), -jnp.inf, f32))
        row_max = jnp.max(mx, axis=-1, keepdims=True)

        def expo(jt, acc, m=m, row_max=row_max):
            off = pl.multiple_of(jt * KV_TILE, KV_TILE)
            e = jnp.exp(s_ref[:, pl.ds(off, KV_TILE)] - row_max)
            e_ref[:, pl.ds(off, KV_TILE)] = e.astype(bf16)
            return acc + (e[:, :LANES] + e[:, LANES:])

        den = lax.fori_loop(0, n_tiles, expo, jnp.zeros((QB, LANES), f32))
        den = jnp.sum(den, axis=-1, keepdims=True)
        pv = jnp.dot(e_ref[...], v_ref[0], preferred_element_type=f32)
        outs.append(pv / den)
    lam = lam_ref[0:1, 0:1]
    o = outs[0] - lam * outs[1]
    o = o * lax.rsqrt(jnp.mean(o * o, axis=-1, keepdims=True) + RMS_EPS) * nw_ref[...]
    o_ref[0] = (o * (1.0 - LAMBDA_INIT)).astype(bf16)


def _diff_attention(proj3, bias_tiles, lam, norm_w, *, qb):
    B, S, _ = proj3.shape
    w = 2 * DIFF_HEAD_DIM
    return pl.pallas_call(
        functools.partial(_diff_kernel, S=S, QB=qb),
        out_shape=jax.ShapeDtypeStruct((B, S, DIFF_HEADS * w), bf16),
        grid=(B, DIFF_HEADS, S // qb),
        in_specs=[pl.BlockSpec((1, qb, w), lambda b, h, i: (b, i, COL_DQ // w + h)),
                  pl.BlockSpec((1, S, w), lambda b, h, i: (b, 0, COL_DK // w + h)),
                  pl.BlockSpec((1, S, w), lambda b, h, i: (b, 0, COL_DV // w + h)),
                  pl.BlockSpec((1, N_BIAS_TILES, LANES, KV_TILE), lambda b, h, i: (h, 0, 0, 0)),
                  pl.BlockSpec((8, LANES), lambda b, h, i: (0, 0)),
                  pl.BlockSpec((1, w), lambda b, h, i: (0, 0))],
        out_specs=pl.BlockSpec((1, qb, w), lambda b, h, i: (b, i, h)),
        scratch_shapes=[pltpu.VMEM((qb, S), f32), pltpu.VMEM((qb, S), bf16)],
        compiler_params=_cparams(("parallel", "parallel", "arbitrary")),
        name="diff_attn",
    )(proj3, proj3, proj3, bias_tiles, lam, norm_w)


def _cross_kernel(q_ref, k_ref, v_ref, o_ref):
    s = lax.dot_general(q_ref[0], k_ref[0], NT_DIMS, preferred_element_type=f32) * (CROSS_HEAD_DIM ** -0.5)
    e = jnp.exp(s - jnp.max(s, axis=-1, keepdims=True))
    p = e / jnp.sum(e, axis=-1, keepdims=True)
    o_ref[0] = jnp.dot(p.astype(bf16), v_ref[0], preferred_element_type=f32).astype(bf16)


def _cross_attention(proj3, kv3, *, qb):
    B, S, _ = proj3.shape
    M = kv3.shape[1]
    w = CROSS_HEAD_DIM
    return pl.pallas_call(
        _cross_kernel,
        out_shape=jax.ShapeDtypeStruct((B, S, CROSS_HEADS * w), bf16),
        grid=(B, CROSS_HEADS, S // qb),
        in_specs=[pl.BlockSpec((1, qb, w), lambda b, h, i: (b, i, COL_CQ // w + h)),
                  pl.BlockSpec((1, M, w), lambda b, h, i: (b, 0, h)),
                  pl.BlockSpec((1, M, w), lambda b, h, i: (b, 0, CROSS_HEADS + h))],
        out_specs=pl.BlockSpec((1, qb, w), lambda b, h, i: (b, i, h)),
        compiler_params=_cparams(("parallel", "parallel", "arbitrary")),
        name="cross_attn",
    )(proj3, kv3, kv3)


def _merge_kernel(yg_ref, yd_ref, yc_ref, g0_ref, g1_ref, g2_ref, wg_ref, wd_ref, wc_ref, o_ref):
    m = g0_ref[...].astype(f32) * jnp.dot(yg_ref[...], wg_ref[...], preferred_element_type=f32)
    m = m + g1_ref[...].astype(f32) * jnp.dot(yd_ref[...], wd_ref[...], preferred_element_type=f32)
    m = m + g2_ref[...].astype(f32) * jnp.dot(yc_ref[...], wc_ref[...], preferred_element_type=f32)
    o_ref[...] = m.astype(bf16)


def _merge(yg, yd, yc, proj2, wg, wd, wc, *, tm):
    T = yg.shape[0]
    gate_blk = COL_GATE // D_MODEL

    def yspec():
        return pl.BlockSpec((tm, GDN_WIDTH), lambda i: (i, 0))

    def gspec(n):
        return pl.BlockSpec((tm, D_MODEL), lambda i, n=n: (i, gate_blk + n))

    def wspec():
        return pl.BlockSpec((GDN_WIDTH, D_MODEL), lambda i: (0, 0), pipeline_mode=pl.Buffered(1))

    return pl.pallas_call(
        _merge_kernel,
        out_shape=jax.ShapeDtypeStruct((T, D_MODEL), bf16),
        grid=(T // tm,),
        in_specs=[yspec(), yspec(), yspec(), gspec(0), gspec(1), gspec(2), wspec(), wspec(), wspec()],
        out_specs=pl.BlockSpec((tm, D_MODEL), lambda i: (i, 0)),
        compiler_params=_cparams(("parallel",)),
        name="merge",
    )(yg, yd, yc, proj2, proj2, proj2, wg, wd, wc)


def _outproj_kernel(m_ref, x_ref, w_ref, g_ref, b_ref, o_ref):
    y = DEEPNORM_ALPHA * x_ref[...] + jnp.dot(m_ref[...], w_ref[...], preferred_element_type=f32)
    o_ref[...] = _layer_norm(y, g_ref[...], b_ref[...])


def _outproj_ln(merged, x2, w_out, g, b, *, tm):
    T = x2.shape[0]
    return pl.pallas_call(
        _outproj_kernel,
        out_shape=jax.ShapeDtypeStruct((T, D_MODEL), f32),
        grid=(T // tm,),
        in_specs=[pl.BlockSpec((tm, D_MODEL), lambda i: (i, 0)),
                  pl.BlockSpec((tm, D_MODEL), lambda i: (i, 0)),
                  pl.BlockSpec((D_MODEL, D_MODEL), lambda i: (0, 0), pipeline_mode=pl.Buffered(1)),
                  pl.BlockSpec((1, D_MODEL), lambda i: (0, 0)),
                  pl.BlockSpec((1, D_MODEL), lambda i: (0, 0))],
        out_specs=pl.BlockSpec((tm, D_MODEL), lambda i: (i, 0)),
        compiler_params=_cparams(("parallel",)),
        name="outproj_ln",
    )(merged, x2, w_out, g, b)


HALO = 16
FFN_ROWS = 128


def _ffn_kernel(xm_ref, xp_ref, xn_ref, wup_ref, cw_ref, wdn_ref, g_ref, b_ref, o_ref, xh_ref, up_ref, h_ref,
                *, tm, tn, tiles_per_seq):
    i = pl.program_id(0)
    n = pl.program_id(1)

    @pl.when(n == 0)
    def _():
        pos = i % tiles_per_seq
        keep_prev = (pos != 0).astype(f32)
        keep_next = (pos != tiles_per_seq - 1).astype(f32)
        xh_ref[0:HALO, :] = (xp_ref[...] * keep_prev).astype(bf16)
        xh_ref[HALO:HALO + tm, :] = xm_ref[...].astype(bf16)
        xh_ref[HALO + tm:2 * HALO + tm, :] = (xn_ref[...] * keep_next).astype(bf16)
        o_ref[...] = DEEPNORM_ALPHA * xm_ref[...]

    up_ref[...] = jnp.dot(xh_ref[...], wup_ref[...], preferred_element_type=f32)
    cw = cw_ref[...]

    def act(r, c):
        r0 = pl.multiple_of(r * FFN_ROWS, FFN_ROWS)
        nwin = FFN_ROWS + 16
        win = up_ref[pl.ds(r0 + HALO - 8, nwin), :]
        acc = (pltpu.roll(win, 1, 0)[8:8 + FFN_ROWS, :] * cw[0:1, :]
               + win[8:8 + FFN_ROWS, :] * cw[1:2, :]
               + pltpu.roll(win, nwin - 1, 0)[8:8 + FFN_ROWS, :] * cw[2:3, :])
        h_ref[pl.ds(r0, FFN_ROWS), :] = (_silu(acc[:, :tn]) * acc[:, tn:]).astype(bf16)
        return c

    lax.fori_loop(0, tm // FFN_ROWS, act, 0)
    o_ref[...] += jnp.dot(h_ref[...], wdn_ref[...], preferred_element_type=f32)

    @pl.when(n == pl.num_programs(1) - 1)
    def _():
        o_ref[...] = _layer_norm(o_ref[...], g_ref[...], b_ref[...])


def _ffn_ln(x1, w_up_r, conv_r, w_down_p, g, b, *, S, tm, tn):
    T = x1.shape[0]
    nf = D_FF_PAD // tn
    hpt = tm // HALO
    last_halo = T // HALO - 1
    return pl.pallas_call(
        functools.partial(_ffn_kernel, tm=tm, tn=tn, tiles_per_seq=S // tm),
        out_shape=jax.ShapeDtypeStruct((T, D_MODEL), f32),
        grid=(T // tm, nf),
        in_specs=[pl.BlockSpec((tm, D_MODEL), lambda i, n: (i, 0)),
                  pl.BlockSpec((HALO, D_MODEL), lambda i, n: (jnp.maximum(i * hpt - 1, 0), 0)),
                  pl.BlockSpec((HALO, D_MODEL), lambda i, n: (jnp.minimum((i + 1) * hpt, last_halo), 0)),
                  pl.BlockSpec((D_MODEL, 2 * tn), lambda i, n: (0, n)),
                  pl.BlockSpec((3, 2 * tn), lambda i, n: (0, n)),
                  pl.BlockSpec((tn, D_MODEL), lambda i, n: (n, 0)),
                  pl.BlockSpec((1, D_MODEL), lambda i, n: (0, 0)),
                  pl.BlockSpec((1, D_MODEL), lambda i, n: (0, 0))],
        out_specs=pl.BlockSpec((tm, D_MODEL), lambda i, n: (i, 0)),
        scratch_shapes=[pltpu.VMEM((tm + 2 * HALO, D_MODEL), bf16),
                        pltpu.VMEM((tm + 2 * HALO, 2 * tn), f32),
                        pltpu.VMEM((tm, tn), bf16)],
        compiler_params=_cparams(("parallel", "arbitrary")),
        name="ffn_ln",
    )(x1, x1, x1, w_up_r, conv_r, w_down_p, g, b)


def _pick(n, prefs):
    for p in prefs:
        if n % p == 0:
            return p
    raise ValueError(f"no tile in {prefs} divides {n}")


def _trunk(x, mem, wts):
    B, S, _ = x.shape
    T = B * S
    assert S % CONV_ROWS == 0 and S % KV_TILE == 0
    x2 = x.reshape(T, D_MODEL)

    proj2, ab2 = _inproj(x2, wts["w_all"], wts["b_all"], wts["w_ab"], tm=_pick(T, (1024, 512, 256)), tn=512)
    proj3 = proj2.reshape(B, S, PROJ_COLS)
    ab3 = ab2.reshape(B, S, LANES)

    y_gdn = _gdn(proj3, ab3, wts["gdn_conv"], wts["a_log"], wts["dt_bias"], wts["gdn_norm_w"])
    y_diff = _diff_attention(proj3, wts["bias_tiles"], wts["lam"], wts["diff_norm_w"], qb=_pick(S, (256,)))
    M = mem.shape[1]
    kv = _matmul(mem.reshape(B * M, D_MODEL), wts["w_mem_kv"], tm=_pick(B * M, (512, 256)))
    y_cross = _cross_attention(proj3, kv.reshape(B, M, 2 * CROSS_HEADS * CROSS_HEAD_DIM), qb=_pick(S, (512, 256)))

    tm = _pick(S, (512, 256))
    merged = _merge(y_gdn.reshape(T, -1), y_diff.reshape(T, -1), y_cross.reshape(T, -1), proj2,
                    wts["w_bg"], wts["w_bd"], wts["w_bc"], tm=tm)
    x1 = _outproj_ln(merged, x2, wts["w_out"], wts["ln1_g"], wts["ln1_b"], tm=tm)
    y = _ffn_ln(x1, wts["w_up_r"], wts["ffn_conv_r"], wts["w_down_p"], wts["ln2_g"], wts["ln2_b"],
                S=S, tm=tm, tn=FFN_TN)
    return y.reshape(B, S, D_MODEL)


FFN_TN = 512


def _prep_weights(rel_bias, w_in, gdn_conv, gdn_a_log, gdn_dt_bias, gdn_norm_w, diff_lambda, diff_norm_w,
                  w_mem_kv, w_gate, b_gate, w_branch_gdn, w_branch_diff, w_branch_cross, w_out,
                  ln1_g, ln1_b, w_up, ffn_conv, w_down, ln2_g, ln2_b):
    l = 0
    wi = w_in[l]
    a0 = 4 * GDN_WIDTH
    a1 = a0 + 4 * GDN_HEADS
    w_lin = jnp.concatenate([wi[:, :a0], wi[:, a1:]], axis=1)
    w_all = jnp.concatenate([w_lin, w_gate[l]], axis=1).astype(bf16)
    b_all = jnp.concatenate([jnp.zeros((PROJ_LIN,), f32), b_gate[l].astype(f32)])[None, :]
    w_ab = jnp.pad(wi[:, a0:a1], ((0, 0), (0, LANES - 4 * GDN_HEADS))).astype(bf16)

    nf = D_FF_PAD // FFN_TN
    padc = D_FF_PAD - D_FF

    def interleave(a):
        gate = jnp.pad(a[:, :D_FF], ((0, 0), (0, padc))).reshape(a.shape[0], nf, FFN_TN)
        val = jnp.pad(a[:, D_FF:], ((0, 0), (0, padc))).reshape(a.shape[0], nf, FFN_TN)
        return jnp.stack([gate, val], axis=2).reshape(a.shape[0], nf * 2 * FFN_TN)

    bias_tiles, lam = _bias_tiles(rel_bias.astype(f32), diff_lambda[l].astype(f32))
    return dict(
        w_all=w_all, b_all=b_all, w_ab=w_ab,
        gdn_conv=gdn_conv[l].astype(f32), a_log=gdn_a_log[l].astype(f32), dt_bias=gdn_dt_bias[l].astype(f32),
        gdn_norm_w=gdn_norm_w[l].astype(f32)[None, :],
        bias_tiles=bias_tiles, lam=lam, diff_norm_w=diff_norm_w[l].astype(f32)[None, :],
        w_mem_kv=w_mem_kv[l].astype(bf16),
        w_bg=w_branch_gdn[l].astype(bf16), w_bd=w_branch_diff[l].astype(bf16), w_bc=w_branch_cross[l].astype(bf16),
        w_out=w_out[l].astype(bf16),
        ln1_g=ln1_g[l].astype(f32)[None, :], ln1_b=ln1_b[l].astype(f32)[None, :],
        w_up_r=interleave(w_up[l]).astype(bf16), ffn_conv_r=interleave(ffn_conv[l].astype(f32)),
        w_down_p=jnp.pad(w_down[l], ((0, padc), (0, 0))).astype(bf16),
        ln2_g=ln2_g[l].astype(f32)[None, :], ln2_b=ln2_b[l].astype(f32)[None, :],
    )


def kernel(x_prompt, x_sample, mem_prompt, mem_sample, rel_bias, w_in, gdn_conv, gdn_a_log, gdn_dt_bias, gdn_norm_w, diff_lambda, diff_norm_w, w_mem_kv, w_gate, b_gate, w_branch_gdn, w_branch_diff, w_branch_cross, w_out, ln1_g, ln1_b, w_up, ffn_conv, w_down, ln2_g, ln2_b):
    wts = _prep_weights(rel_bias, w_in, gdn_conv, gdn_a_log, gdn_dt_bias, gdn_norm_w, diff_lambda, diff_norm_w,
                        w_mem_kv, w_gate, b_gate, w_branch_gdn, w_branch_diff, w_branch_cross, w_out,
                        ln1_g, ln1_b, w_up, ffn_conv, w_down, ln2_g, ln2_b)
    return (_trunk(x_prompt, mem_prompt, wts), _trunk(x_sample, mem_sample, wts))
```

```python
import functools
import math

import jax
import jax.numpy as jnp
import numpy as np
from jax import lax
from jax.experimental import pallas as pl
from jax.experimental.pallas import tpu as pltpu

f32 = jnp.float32
bf16 = jnp.bfloat16

D_MODEL = 2048
GDN_HEADS = 8
GDN_HEAD_DIM = 128
GDN_WIDTH = 1024
GDN_CONV = 5
GDN_CHUNK = 64
DIFF_HEADS = 4
DIFF_HEAD_DIM = 128
CROSS_HEADS = 4
CROSS_HEAD_DIM = 256
N_BRANCH = 3
D_FF = 5504
NUM_BUCKETS = 32
MAX_DISTANCE = 128
LN_EPS = 1e-5
RMS_EPS = 1e-6
L2_EPS = 1e-6
DEPTH = 1
DEEPNORM_ALPHA = (2 * DEPTH) ** 0.25
LAMBDA_INIT = 0.8 - 0.6 * math.exp(-0.3 * 0)

LANES = 128
MXU_WIDTH = 256
VMEM_LIMIT_BYTES = 56 * 1024 * 1024

PROJ_LIN = 8192
COL_GQ, COL_GK, COL_GV, COL_GZ = 0, 1024, 2048, 3072
COL_DQ, COL_DK, COL_DV, COL_CQ = 4096, 5120, 6144, 7168
COL_GATE = PROJ_LIN
PROJ_COLS = PROJ_LIN + N_BRANCH * D_MODEL
D_FF_PAD = 5632

NT_DIMS = (((1,), (1,)), ((), ()))


def _cparams(sem, vmem=VMEM_LIMIT_BYTES):
    return pltpu.CompilerParams(dimension_semantics=sem, vmem_limit_bytes=vmem)


def _sigmoid(x):
    return 1.0 / (1.0 + jnp.exp(-x))


def _silu(x):
    return x * _sigmoid(x)


def _layer_norm(y, g, b):
    mu = jnp.mean(y, axis=-1, keepdims=True)
    yc = y - mu
    var = jnp.mean(yc * yc, axis=-1, keepdims=True)
    return yc * lax.rsqrt(var + LN_EPS) * g + b


def _bdot(a, b):
    return jnp.dot(a.astype(bf16), b.astype(bf16), preferred_element_type=f32)


def _inproj_kernel(x_ref, w_ref, b_ref, wab_ref, o_ref, ab_ref, xb_ref, *, n_lin_tiles):
    j = pl.program_id(1)

    @pl.when(j == 0)
    def _():
        xb = x_ref[...].astype(bf16)
        xb_ref[...] = xb
        ab_ref[...] = jnp.dot(xb, wab_ref[...], preferred_element_type=f32)

    acc = jnp.dot(xb_ref[...], w_ref[...], preferred_element_type=f32)

    @pl.when(j < n_lin_tiles)
    def _():
        o_ref[...] = acc.astype(bf16)

    @pl.when(j >= n_lin_tiles)
    def _():
        o_ref[...] = _sigmoid(acc + b_ref[...]).astype(bf16)


def _inproj(x2, w_all, b_all, w_ab, *, tm, tn):
    T = x2.shape[0]
    n_lin_tiles = PROJ_LIN // tn
    return pl.pallas_call(
        functools.partial(_inproj_kernel, n_lin_tiles=n_lin_tiles),
        out_shape=(jax.ShapeDtypeStruct((T, PROJ_COLS), bf16),
                   jax.ShapeDtypeStruct((T, LANES), f32)),
        grid=(T // tm, PROJ_COLS // tn),
        in_specs=[pl.BlockSpec((tm, D_MODEL), lambda i, j: (i, 0)),
                  pl.BlockSpec((D_MODEL, tn), lambda i, j: (0, j)),
                  pl.BlockSpec((1, tn), lambda i, j: (0, j)),
                  pl.BlockSpec((D_MODEL, LANES), lambda i, j: (0, 0))],
        out_specs=(pl.BlockSpec((tm, tn), lambda i, j: (i, j)),
                   pl.BlockSpec((tm, LANES), lambda i, j: (i, 0))),
        scratch_shapes=[pltpu.VMEM((tm, D_MODEL), bf16)],
        compiler_params=_cparams(("parallel", "arbitrary")),
        name="inproj",
    )(x2, w_all, b_all, w_ab)


def _mm_kernel(x_ref, w_ref, o_ref):
    o_ref[...] = jnp.dot(x_ref[...].astype(bf16), w_ref[...], preferred_element_type=f32).astype(o_ref.dtype)


def _matmul(x2, w, *, tm, out_dtype=bf16):
    T, K = x2.shape
    N = w.shape[1]
    return pl.pallas_call(
        _mm_kernel,
        out_shape=jax.ShapeDtypeStruct((T, N), out_dtype),
        grid=(T // tm,),
        in_specs=[pl.BlockSpec((tm, K), lambda i: (i, 0)),
                  pl.BlockSpec((K, N), lambda i: (0, 0), pipeline_mode=pl.Buffered(1))],
        out_specs=pl.BlockSpec((tm, N), lambda i: (i, 0)),
        compiler_params=_cparams(("parallel",)),
        name="mem_kv",
    )(x2, w)


SB = 2 * GDN_CHUNK
CONV_ROWS = 256
CONV_PAD = 8
CHUNK_LHS = GDN_HEAD_DIM + GDN_CHUNK
GDN_UNROLL = 2
INV_BASE = 8


def _split_bf16(x):
    hi = x.astype(bf16)
    lo = (x - hi.astype(f32)).astype(bf16)
    return hi, lo


def _gdn_kernel(alog_ref, dtb_ref, q_ref, k_ref, v_ref, z_ref, ab_ref, cq_ref, ck_ref, cv_ref,
                nw_ref, o_ref, qn_ref, kn_ref, vn_ref, xp_ref, lhs_ref, n_ref, egl_ref, oacc_ref, *, S, unroll):
    h = pl.program_id(1)
    nsb = S // SB
    hd = GDN_HEAD_DIM

    zpad = jnp.zeros((CONV_PAD, hd), f32)
    xp_ref[0:CONV_PAD, :] = zpad
    xp_ref[S + CONV_PAD:S + 2 * CONV_PAD, :] = zpad

    def conv_into(src_ref, w_ref, dst_ref, mode):
        xp_ref[CONV_PAD:S + CONV_PAD, :] = src_ref[0].astype(f32)
        w = w_ref[...]

        def body(r, c):
            r0 = pl.multiple_of(r * CONV_ROWS, CONV_ROWS)
            acc = jnp.zeros((CONV_ROWS, hd), f32)
            for j in range(GDN_CONV):
                acc = acc + xp_ref[pl.ds(r0 + CONV_PAD - GDN_CONV // 2 + j, CONV_ROWS), :] * w[j:j + 1, :]
            y = _silu(acc)
            if mode != "v":
                y = y * lax.rsqrt(jnp.sum(y * y, axis=-1, keepdims=True) + L2_EPS)
            if mode == "q":
                y = y * (hd ** -0.5)
            dst_ref[pl.ds(r0, CONV_ROWS), :] = y
            return c

        lax.fori_loop(0, S // CONV_ROWS, body, 0)

    conv_into(q_ref, cq_ref, qn_ref, "q")
    conv_into(k_ref, ck_ref, kn_ref, "k")
    conv_into(v_ref, cv_ref, vn_ref, "v")

    ri = lax.broadcasted_iota(jnp.int32, (SB, SB), 0)
    ci = lax.broadcasted_iota(jnp.int32, (SB, SB), 1)

    def same(blk):
        return (ri // blk) == (ci // blk)

    same_chunk = same(GDN_CHUNK)
    incl = (same_chunk & (ri >= ci), same_chunk & (ri <= ci))
    strict = (same_chunk & (ri > ci), same_chunk & (ri < ci))
    same_base = same(INV_BASE)
    merge_masks = []
    blk = INV_BASE
    while blk < GDN_CHUNK:
        merge_masks.append(same(2 * blk) & jnp.logical_not(same(blk)))
        blk *= 2
    eye = (ri == ci).astype(f32)
    tri2 = [jnp.concatenate([incl[d].astype(bf16)] * 2, axis=1) for d in range(2)]
    r4 = lax.broadcasted_iota(jnp.int32, (2 * SB, 4 * hd), 0) % SB
    c4 = lax.broadcasted_iota(jnp.int32, (2 * SB, 4 * hd), 1) // hd
    sel = (r4 == (c4 * GDN_HEADS + h)).astype(bf16)
    rh = lax.broadcasted_iota(jnp.int32, (SB, 4 * hd), 0) < GDN_CHUNK
    ch = lax.broadcasted_iota(jnp.int32, (SB, 4 * hd), 1) < 2 * hd
    half_mask = rh == ch

    def phase_a(i, c):
        chains = []
        for u in range(unroll):
            sb = i * unroll + u
            rows = pl.ds(pl.multiple_of(sb * SB, SB), SB)
            kn = kn_ref[rows, :]
            qn = qn_ref[rows, :]
            vn = vn_ref[rows, :]
            knb = kn.astype(bf16)
            kq = lax.dot_general(jnp.concatenate([knb, qn.astype(bf16)], axis=0), knb, NT_DIMS,
                                 preferred_element_type=f32)
            ab_hi, ab_lo = _split_bf16(ab_ref[0, rows, :])
            absel = jnp.dot(jnp.concatenate([ab_hi, ab_lo], axis=1), sel, preferred_element_type=f32)
            for d in range(2):
                chains.append(dict(sb=sb, rows=rows, d=d, kn=kn, qn=qn, vn=vn, kk=kq[:SB], qk=kq[SB:],
                                   a=absel[:, d * hd:(d + 1) * hd], b=absel[:, (2 + d) * hd:(3 + d) * hd]))
        for t in chains:
            d = t["d"]
            xs = t["a"] + dtb_ref[d, h]
            softplus = jnp.maximum(xs, 0.0) + jnp.log1p(jnp.exp(-jnp.abs(xs)))
            g = -jnp.exp(jnp.full((1, hd), alog_ref[d, h], f32)) * softplus
            t["beta"] = _sigmoid(t["b"])
            g_hi, g_lo = _split_bf16(g)
            t["gc"] = jnp.dot(tri2[d], jnp.concatenate([g_hi, g_lo], axis=0),
                              preferred_element_type=f32)
        for t in chains:
            d, gc, beta = t["d"], t["gc"], t["beta"]
            diff = gc - gc.T
            t["decay"] = jnp.where(incl[d], jnp.exp(jnp.where(incl[d], diff, 0.0)), 0.0)
            t["l"] = jnp.where(strict[d], beta * t["kk"] * t["decay"], 0.0)
            t["egc"] = jnp.exp(gc)
            t["rhs"] = jnp.concatenate([beta * t["vn"], (beta * t["egc"]) * t["kn"]], axis=1)
            t["ld"] = jnp.where(same_base, t["l"], 0.0)
            t["ld2"] = _bdot(t["ld"], t["ld"])
        for t in chains:
            iml = eye - t["ld"]
            t["x"] = iml + _bdot(iml, t["ld2"])
            t["ld4"] = _bdot(t["ld2"], t["ld2"])
        for t in chains:
            t["t"] = t["x"] + _bdot(t["x"], t["ld4"])
        for mask in merge_masks:
            for t in chains:
                t["w"] = _bdot(t["t"], jnp.where(mask, t["l"], 0.0))
            for t in chains:
                t["t"] = t["t"] - _bdot(t["w"], t["t"])
        for t in chains:
            t["sol"] = _bdot(t["t"], t["rhs"])
        for t in chains:
            d, gc, sol = t["d"], t["gc"], t["sol"]
            w = jnp.concatenate([sol[:, hd:], sol[:, :hd]], axis=1).astype(bf16)
            qkd = jnp.where(incl[d], t["qk"] * t["decay"], 0.0).astype(bf16)
            t["qw"] = jnp.dot(qkd, w, preferred_element_type=f32)
            last = (GDN_CHUNK - 1) if d == 0 else 0
            gl = jnp.concatenate(
                [jnp.broadcast_to(gc[last:last + 1, :], (GDN_CHUNK, hd)),
                 jnp.broadcast_to(gc[GDN_CHUNK + last:GDN_CHUNK + last + 1, :], (GDN_CHUNK, hd))], axis=0)
            kdt = (t["kn"] * jnp.exp(gl - gc)).T.astype(bf16)
            wpair = jnp.where(half_mask, jnp.concatenate([w, w], axis=1), jnp.zeros((SB, 4 * hd), bf16))
            t["kw"] = jnp.dot(kdt, wpair, preferred_element_type=f32)
            t["egl"] = jnp.exp(gl)
        for t in chains:
            d, sb = t["d"], t["sb"]
            qp = (t["qn"] * t["egc"] - t["qw"][:, :hd]).astype(bf16)
            for half in range(2):
                cidx = sb * 2 + half
                l0 = pl.multiple_of(cidx * CHUNK_LHS, CHUNK_LHS)
                lhs_ref[d, pl.ds(l0, hd), :] = (-t["kw"][:, 2 * half * hd:(2 * half + 1) * hd]).astype(bf16)
                lhs_ref[d, pl.ds(l0 + hd, GDN_CHUNK), :] = qp[half * GDN_CHUNK:(half + 1) * GDN_CHUNK, :]
                n_ref[d, pl.ds(pl.multiple_of(cidx * hd, hd), hd), :] = t["kw"][:, (2 * half + 1) * hd:(2 * half + 2) * hd]
                egl_ref[d, pl.ds(pl.multiple_of(sb * 16, 16) + 8 * half, 8), :] = t["egl"][half * GDN_CHUNK:half * GDN_CHUNK + 8, :]
        for tf, tb in zip(chains[0::2], chains[1::2]):
            oacc_ref[tf["rows"], :] = tf["qw"][:, hd:] + tb["qw"][:, hd:]
        return c

    lax.fori_loop(0, nsb // unroll, phase_a, 0)

    def chunk_step(d, sbi, half, state):
        cidx = sbi * 2 + half
        lhs = lhs_ref[d, pl.ds(pl.multiple_of(cidx * CHUNK_LHS, CHUNK_LHS), CHUNK_LHS), :]
        pop = jnp.dot(lhs, state.astype(bf16), preferred_element_type=f32)
        rows = pl.ds(pl.multiple_of(sbi * SB, SB) + half * GDN_CHUNK, GDN_CHUNK)
        oacc_ref[rows, :] += pop[hd:, :]
        egl = egl_ref[d, pl.ds(pl.multiple_of(sbi * 16, 16) + 8 * half, 8), :][0:1, :]
        return egl * state + pop[:hd, :] + n_ref[d, pl.ds(pl.multiple_of(cidx * hd, hd), hd), :]

    def phase_b(t, carry):
        sf, sbk = carry
        sf = chunk_step(0, t, 0, sf)
        sbk = chunk_step(1, nsb - 1 - t, 1, sbk)
        sf = chunk_step(0, t, 1, sf)
        sbk = chunk_step(1, nsb - 1 - t, 0, sbk)
        return sf, sbk

    z0 = jnp.zeros((hd, hd), f32)
    lax.fori_loop(0, nsb, phase_b, (z0, z0))

    nw = nw_ref[...]

    def fin(r, c):
        rows = pl.ds(pl.multiple_of(r * CONV_ROWS, CONV_ROWS), CONV_ROWS)
        o = oacc_ref[rows, :]
        o = o * lax.rsqrt(jnp.mean(o * o, axis=-1, keepdims=True) + RMS_EPS) * nw
        o_ref[0, rows, :] = (o * _silu(z_ref[0, rows, :].astype(f32))).astype(bf16)
        return c

    lax.fori_loop(0, S // CONV_ROWS, fin, 0)


def _gdn(proj3, ab3, conv_w, a_log, dt_bias, norm_w):
    B, S, _ = proj3.shape
    hb = GDN_HEAD_DIM
    nsb = S // SB

    def col(base):
        return pl.BlockSpec((1, S, hb), lambda b, h, base=base: (b, 0, base // hb + h))

    def cw(base):
        return pl.BlockSpec((GDN_CONV, hb), lambda b, h, base=base: (0, base // hb + h))

    smem = pl.BlockSpec(memory_space=pltpu.SMEM)
    return pl.pallas_call(
        functools.partial(_gdn_kernel, S=S, unroll=GDN_UNROLL),
        out_shape=jax.ShapeDtypeStruct((B, S, GDN_WIDTH), bf16),
        grid=(B, GDN_HEADS),
        in_specs=[smem, smem, col(COL_GQ), col(COL_GK), col(COL_GV), col(COL_GZ),
                  pl.BlockSpec((1, S, LANES), lambda b, h: (b, 0, 0)),
                  cw(0), cw(GDN_WIDTH), cw(2 * GDN_WIDTH),
                  pl.BlockSpec((1, hb), lambda b, h: (0, 0))],
        out_specs=pl.BlockSpec((1, S, hb), lambda b, h: (b, 0, h)),
        scratch_shapes=[pltpu.VMEM((S, hb), f32), pltpu.VMEM((S, hb), f32), pltpu.VMEM((S, hb), f32),
                        pltpu.VMEM((S + 2 * CONV_PAD, hb), f32),
                        pltpu.VMEM((2, 2 * nsb * CHUNK_LHS, hb), bf16),
                        pltpu.VMEM((2, 2 * nsb * hb, hb), f32),
                        pltpu.VMEM((2, nsb * 16, hb), f32),
                        pltpu.VMEM((S, hb), f32)],
        compiler_params=_cparams(("parallel", "arbitrary")),
        name="gdn",
    )(a_log, dt_bias, proj3, proj3, proj3, proj3, ab3, conv_w, conv_w, conv_w, norm_w)


N_BIAS_TILES = 6
KV_TILE = 2 * LANES


def _bias_kernel(rb_ref, bucket_ref, lp_ref, bias_ref, lam_ref):
    bucket = bucket_ref[...]
    for hh in range(DIFF_HEADS):
        acc = jnp.zeros(bucket.shape, f32)
        for bk in range(NUM_BUCKETS):
            acc = jnp.where(bucket == bk, rb_ref[bk, hh], acc)
        bias_ref[hh] = acc
    lp = lp_ref[...]
    lam = (jnp.exp(jnp.sum(lp[0:1] * lp[1:2], axis=-1, keepdims=True))
           - jnp.exp(jnp.sum(lp[2:3] * lp[3:4], axis=-1, keepdims=True)) + LAMBDA_INIT)
    lam_ref[...] = jnp.broadcast_to(lam, lam_ref.shape)


def _t5_bucket(rel):
    nb = NUM_BUCKETS // 2
    max_exact = nb // 2
    ret = jnp.where(rel > 0, nb, 0)
    n = jnp.abs(rel)
    nf = jnp.maximum(n, 1).astype(jnp.float32)
    large = max_exact + (jnp.log(nf / max_exact) / math.log(MAX_DISTANCE / max_exact)
                         * (nb - max_exact)).astype(jnp.int32)
    large = jnp.minimum(large, nb - 1)
    return ret + jnp.where(n < max_exact, n, large)


def _bias_tiles(rel_bias, lam_params):
    delta = jnp.arange(N_BIAS_TILES, dtype=jnp.int32)[:, None, None] - 3
    rel = (LANES * delta + jnp.arange(KV_TILE, dtype=jnp.int32)[None, None, :]
           - jnp.arange(LANES, dtype=jnp.int32)[None, :, None])
    bucket = _t5_bucket(rel).astype(jnp.int32)
    return pl.pallas_call(
        _bias_kernel,
        out_shape=(jax.ShapeDtypeStruct((DIFF_HEADS, N_BIAS_TILES, LANES, KV_TILE), f32),
                   jax.ShapeDtypeStruct((8, LANES), f32)),
        in_specs=[pl.BlockSpec(memory_space=pltpu.SMEM),
                  pl.BlockSpec(memory_space=pltpu.VMEM),
                  pl.BlockSpec(memory_space=pltpu.VMEM)],
        out_specs=(pl.BlockSpec(memory_space=pltpu.VMEM), pl.BlockSpec(memory_space=pltpu.VMEM)),
        name="rel_bias",
    )(rel_bias, bucket, lam_params)


def _diff_kernel(q_ref, k_ref, v_ref, bias_ref, lam_ref, nw_ref, o_ref, s_ref, e_ref, *, S, QB):
    qi = pl.program_id(2)
    n_tiles = S // KV_TILE
    nsub = QB // LANES
    dh = DIFF_HEAD_DIM
    q = (q_ref[0].astype(f32) * (dh ** -0.5)).astype(bf16)

    def tile(jt):
        return slice(jt * KV_TILE, (jt + 1) * KV_TILE)

    def score_tile(m, jt, mx):
        s = lax.dot_general(q[:, m * dh:(m + 1) * dh], k_ref[0, tile(jt), m * dh:(m + 1) * dh], NT_DIMS,
                            preferred_element_type=f32)
        parts = []
        for r in range(nsub):
            idx = jnp.clip(2 * jt - (qi * nsub + r), -3, 2) + 3
            parts.append(s[r * LANES:(r + 1) * LANES, :] + bias_ref[0, idx])
        s = jnp.concatenate(parts, axis=0) if nsub > 1 else parts[0]
        s_ref[m, :, tile(jt)] = s
        return jnp.maximum(mx, jnp.maximum(s[:, :LANES], s[:, LANES:]))

    def exp_tile(m, jt, row_max, den):
        e = jnp.exp(s_ref[m, :, tile(jt)] - row_max)
        e_ref[m, :, tile(jt)] = e.astype(bf16)
        return den + (e[:, :LANES] + e[:, LANES:])

    def pv_tile(m, jt, acc):
        return acc + jnp.dot(e_ref[m, :, tile(jt)], v_ref[0, tile(jt), :], preferred_element_type=f32)

    neg = jnp.full((QB, LANES), -jnp.inf, f32)
    zero = jnp.zeros((QB, LANES), f32)
    mx0 = neg
    for jt in range(n_tiles):
        mx0 = score_tile(0, jt, mx0)
    max0 = jnp.max(mx0, axis=-1, keepdims=True)
    mx1, den0 = neg, zero
    for jt in range(n_tiles):
        mx1 = score_tile(1, jt, mx1)
        den0 = exp_tile(0, jt, max0, den0)
    max1 = jnp.max(mx1, axis=-1, keepdims=True)
    pv0, den1 = jnp.zeros((QB, 2 * dh), f32), zero
    for jt in range(n_tiles):
        pv0 = pv_tile(0, jt, pv0)
        den1 = exp_tile(1, jt, max1, den1)
    pv1 = jnp.zeros((QB, 2 * dh), f32)
    for jt in range(n_tiles):
        pv1 = pv_tile(1, jt, pv1)
    outs = [pv0 / jnp.sum(den0, axis=-1, keepdims=True), pv1 / jnp.sum(den1, axis=-1, keepdims=True)]
    lam = lam_ref[0:1, 0:1]
    o = outs[0] - lam * outs[1]
    o = o * lax.rsqrt(jnp.mean(o * o, axis=-1, keepdims=True) + RMS_EPS) * nw_ref[...]
    o_ref[0] = (o * (1.0 - LAMBDA_INIT)).astype(bf16)


def _diff_attention(proj3, bias_tiles, lam, norm_w, *, qb):
    B, S, _ = proj3.shape
    w = 2 * DIFF_HEAD_DIM
    return pl.pallas_call(
        functools.partial(_diff_kernel, S=S, QB=qb),
        out_shape=jax.ShapeDtypeStruct((B, S, DIFF_HEADS * w), bf16),
        grid=(B, DIFF_HEADS, S // qb),
        in_specs=[pl.BlockSpec((1, qb, w), lambda b, h, i: (b, i, COL_DQ // w + h)),
                  pl.BlockSpec((1, S, w), lambda b, h, i: (b, 0, COL_DK // w + h)),
                  pl.BlockSpec((1, S, w), lambda b, h, i: (b, 0, COL_DV // w + h)),
                  pl.BlockSpec((1, N_BIAS_TILES, LANES, KV_TILE), lambda b, h, i: (h, 0, 0, 0)),
                  pl.BlockSpec((8, LANES), lambda b, h, i: (0, 0)),
                  pl.BlockSpec((1, w), lambda b, h, i: (0, 0))],
        out_specs=pl.BlockSpec((1, qb, w), lambda b, h, i: (b, i, h)),
        scratch_shapes=[pltpu.VMEM((2, qb, S), f32), pltpu.VMEM((2, qb, S), bf16)],
        compiler_params=_cparams(("parallel", "parallel", "arbitrary")),
        name="diff_attn",
    )(proj3, proj3, proj3, bias_tiles, lam, norm_w)


def _cross_kernel(q_ref, k_ref, v_ref, o_ref):
    s = lax.dot_general(q_ref[0], k_ref[0], NT_DIMS, preferred_element_type=f32) * (CROSS_HEAD_DIM ** -0.5)
    e = jnp.exp(s - jnp.max(s, axis=-1, keepdims=True))
    p = e / jnp.sum(e, axis=-1, keepdims=True)
    o_ref[0] = jnp.dot(p.astype(bf16), v_ref[0], preferred_element_type=f32).astype(bf16)


def _cross_attention(proj3, kv3, *, qb):
    B, S, _ = proj3.shape
    M = kv3.shape[1]
    w = CROSS_HEAD_DIM
    return pl.pallas_call(
        _cross_kernel,
        out_shape=jax.ShapeDtypeStruct((B, S, CROSS_HEADS * w), bf16),
        grid=(B, CROSS_HEADS, S // qb),
        in_specs=[pl.BlockSpec((1, qb, w), lambda b, h, i: (b, i, COL_CQ // w + h)),
                  pl.BlockSpec((1, M, w), lambda b, h, i: (b, 0, h)),
                  pl.BlockSpec((1, M, w), lambda b, h, i: (b, 0, CROSS_HEADS + h))],
        out_specs=pl.BlockSpec((1, qb, w), lambda b, h, i: (b, i, h)),
        compiler_params=_cparams(("parallel", "parallel", "arbitrary")),
        name="cross_attn",
    )(proj3, kv3, kv3)


def _merge_kernel(yg_ref, yd_ref, yc_ref, g0_ref, g1_ref, g2_ref, wg_ref, wd_ref, wc_ref, o_ref):
    m = g0_ref[...].astype(f32) * jnp.dot(yg_ref[...], wg_ref[...], preferred_element_type=f32)
    m = m + g1_ref[...].astype(f32) * jnp.dot(yd_ref[...], wd_ref[...], preferred_element_type=f32)
    m = m + g2_ref[...].astype(f32) * jnp.dot(yc_ref[...], wc_ref[...], preferred_element_type=f32)
    o_ref[...] = m.astype(bf16)


def _merge(yg, yd, yc, proj2, wg, wd, wc, *, tm):
    T = yg.shape[0]
    gate_blk = COL_GATE // D_MODEL

    def yspec():
        return pl.BlockSpec((tm, GDN_WIDTH), lambda i: (i, 0))

    def gspec(n):
        return pl.BlockSpec((tm, D_MODEL), lambda i, n=n: (i, gate_blk + n))

    def wspec():
        return pl.BlockSpec((GDN_WIDTH, D_MODEL), lambda i: (0, 0), pipeline_mode=pl.Buffered(1))

    return pl.pallas_call(
        _merge_kernel,
        out_shape=jax.ShapeDtypeStruct((T, D_MODEL), bf16),
        grid=(T // tm,),
        in_specs=[yspec(), yspec(), yspec(), gspec(0), gspec(1), gspec(2), wspec(), wspec(), wspec()],
        out_specs=pl.BlockSpec((tm, D_MODEL), lambda i: (i, 0)),
        compiler_params=_cparams(("parallel",)),
        name="merge",
    )(yg, yd, yc, proj2, proj2, proj2, wg, wd, wc)


def _outproj_kernel(m_ref, x_ref, w_ref, g_ref, b_ref, o_ref):
    y = DEEPNORM_ALPHA * x_ref[...] + jnp.dot(m_ref[...], w_ref[...], preferred_element_type=f32)
    o_ref[...] = _layer_norm(y, g_ref[...], b_ref[...])


def _outproj_ln(merged, x2, w_out, g, b, *, tm):
    T = x2.shape[0]
    return pl.pallas_call(
        _outproj_kernel,
        out_shape=jax.ShapeDtypeStruct((T, D_MODEL), f32),
        grid=(T // tm,),
        in_specs=[pl.BlockSpec((tm, D_MODEL), lambda i: (i, 0)),
                  pl.BlockSpec((tm, D_MODEL), lambda i: (i, 0)),
                  pl.BlockSpec((D_MODEL, D_MODEL), lambda i: (0, 0), pipeline_mode=pl.Buffered(1)),
                  pl.BlockSpec((1, D_MODEL), lambda i: (0, 0)),
                  pl.BlockSpec((1, D_MODEL), lambda i: (0, 0))],
        out_specs=pl.BlockSpec((tm, D_MODEL), lambda i: (i, 0)),
        compiler_params=_cparams(("parallel",)),
        name="outproj_ln",
    )(merged, x2, w_out, g, b)


HALO = 16
FFN_TN = 512
FFN_SUB = 128


def _ffn_kernel(xm_ref, xp_ref, xn_ref, wup_ref, cw_ref, wdn_ref, g_ref, b_ref, o_ref, xh_ref, h_ref,
                *, tm, tn, tiles_per_seq):
    i = pl.program_id(0)
    n = pl.program_id(1)

    @pl.when(n == 0)
    def _():
        pos = i % tiles_per_seq
        keep_prev = (pos != 0).astype(f32)
        keep_next = (pos != tiles_per_seq - 1).astype(f32)
        xh_ref[0:HALO, :] = (xp_ref[...] * keep_prev).astype(bf16)
        xh_ref[HALO:HALO + tm, :] = xm_ref[...].astype(bf16)
        xh_ref[HALO + tm:2 * HALO + tm, :] = (xn_ref[...] * keep_next).astype(bf16)
        o_ref[...] = DEEPNORM_ALPHA * xm_ref[...]

    cw = cw_ref[...]
    sub = 2 * FFN_SUB
    rows = tm + 2 * HALO

    def up_dot(c):
        return jnp.dot(xh_ref[...], wup_ref[:, c * sub:(c + 1) * sub], preferred_element_type=f32)

    def act(c, up):
        w = cw[:, c * sub:(c + 1) * sub]
        acc = (pltpu.roll(up, 1, 0)[HALO:HALO + tm, :] * w[0:1, :]
               + up[HALO:HALO + tm, :] * w[1:2, :]
               + pltpu.roll(up, rows - 1, 0)[HALO:HALO + tm, :] * w[2:3, :])
        h_ref[:, c * FFN_SUB:(c + 1) * FFN_SUB] = (_silu(acc[:, :FFN_SUB]) * acc[:, FFN_SUB:]).astype(bf16)

    nsub = tn // FFN_SUB
    up = up_dot(0)
    for c in range(nsub):
        nxt = up_dot(c + 1) if c + 1 < nsub else None
        act(c, up)
        up = nxt
    o_ref[...] += jnp.dot(h_ref[...], wdn_ref[...], preferred_element_type=f32)

    @pl.when(n == pl.num_programs(1) - 1)
    def _():
        o_ref[...] = _layer_norm(o_ref[...], g_ref[...], b_ref[...])


def _ffn_ln(x1, w_up_r, conv_r, w_down_p, g, b, *, S, tm, tn):
    T = x1.shape[0]
    nf = D_FF_PAD // tn
    hpt = tm // HALO
    last_halo = T // HALO - 1
    return pl.pallas_call(
        functools.partial(_ffn_kernel, tm=tm, tn=tn, tiles_per_seq=S // tm),
        out_shape=jax.ShapeDtypeStruct((T, D_MODEL), f32),
        grid=(T // tm, nf),
        in_specs=[pl.BlockSpec((tm, D_MODEL), lambda i, n: (i, 0)),
                  pl.BlockSpec((HALO, D_MODEL), lambda i, n: (jnp.maximum(i * hpt - 1, 0), 0)),
                  pl.BlockSpec((HALO, D_MODEL), lambda i, n: (jnp.minimum((i + 1) * hpt, last_halo), 0)),
                  pl.BlockSpec((D_MODEL, 2 * tn), lambda i, n: (0, n)),
                  pl.BlockSpec((3, 2 * tn), lambda i, n: (0, n)),
                  pl.BlockSpec((tn, D_MODEL), lambda i, n: (n, 0)),
                  pl.BlockSpec((1, D_MODEL), lambda i, n: (0, 0)),
                  pl.BlockSpec((1, D_MODEL), lambda i, n: (0, 0))],
        out_specs=pl.BlockSpec((tm, D_MODEL), lambda i, n: (i, 0)),
        scratch_shapes=[pltpu.VMEM((tm + 2 * HALO, D_MODEL), bf16),
                        pltpu.VMEM((tm, tn), bf16)],
        compiler_params=_cparams(("parallel", "arbitrary")),
        name="ffn_ln",
    )(x1, x1, x1, w_up_r, conv_r, w_down_p, g, b)


def _pick(n, prefs):
    for p in prefs:
        if n % p == 0:
            return p
    raise ValueError(f"no tile in {prefs} divides {n}")


def _trunk(x, mem, wts):
    B, S, _ = x.shape
    T = B * S
    assert S % CONV_ROWS == 0 and S % KV_TILE == 0 and (S // SB) % GDN_UNROLL == 0
    x2 = x.reshape(T, D_MODEL)

    proj2, ab2 = _inproj(x2, wts["w_all"], wts["b_all"], wts["w_ab"], tm=_pick(T, (1024, 512, 256)), tn=512)
    proj3 = proj2.reshape(B, S, PROJ_COLS)
    ab3 = ab2.reshape(B, S, LANES)

    y_gdn = _gdn(proj3, ab3, wts["gdn_conv"], wts["a_log"], wts["dt_bias"], wts["gdn_norm_w"])
    y_diff = _diff_attention(proj3, wts["bias_tiles"], wts["lam"], wts["diff_norm_w"], qb=_pick(S, (256,)))
    M = mem.shape[1]
    kv = _matmul(mem.reshape(B * M, D_MODEL), wts["w_mem_kv"], tm=_pick(B * M, (512, 256)))
    y_cross = _cross_attention(proj3, kv.reshape(B, M, 2 * CROSS_HEADS * CROSS_HEAD_DIM), qb=_pick(S, (512, 256)))

    tm = _pick(S, (512, 256))
    merged = _merge(y_gdn.reshape(T, -1), y_diff.reshape(T, -1), y_cross.reshape(T, -1), proj2,
                    wts["w_bg"], wts["w_bd"], wts["w_bc"], tm=tm)
    x1 = _outproj_ln(merged, x2, wts["w_out"], wts["ln1_g"], wts["ln1_b"], tm=tm)
    y = _ffn_ln(x1, wts["w_up_r"], wts["ffn_conv_r"], wts["w_down_p"], wts["ln2_g"], wts["ln2_b"],
                S=S, tm=tm, tn=FFN_TN)
    return y.reshape(B, S, D_MODEL)


def _prep_weights(rel_bias, w_in, gdn_conv, gdn_a_log, gdn_dt_bias, gdn_norm_w, diff_lambda, diff_norm_w,
                  w_mem_kv, w_gate, b_gate, w_branch_gdn, w_branch_diff, w_branch_cross, w_out,
                  ln1_g, ln1_b, w_up, ffn_conv, w_down, ln2_g, ln2_b):
    l = 0
    wi = w_in[l]
    a0 = 4 * GDN_WIDTH
    a1 = a0 + 4 * GDN_HEADS
    w_lin = jnp.concatenate([wi[:, :a0], wi[:, a1:]], axis=1)
    w_all = jnp.concatenate([w_lin, w_gate[l]], axis=1).astype(bf16)
    b_all = jnp.concatenate([jnp.zeros((PROJ_LIN,), f32), b_gate[l].astype(f32)])[None, :]
    w_ab = jnp.pad(wi[:, a0:a1], ((0, 0), (0, LANES - 4 * GDN_HEADS))).astype(bf16)

    nsub = D_FF_PAD // FFN_SUB
    padc = D_FF_PAD - D_FF

    def interleave(a):
        gate = jnp.pad(a[:, :D_FF], ((0, 0), (0, padc))).reshape(a.shape[0], nsub, FFN_SUB)
        val = jnp.pad(a[:, D_FF:], ((0, 0), (0, padc))).reshape(a.shape[0], nsub, FFN_SUB)
        return jnp.stack([gate, val], axis=2).reshape(a.shape[0], nsub * 2 * FFN_SUB)

    bias_tiles, lam = _bias_tiles(rel_bias.astype(f32), diff_lambda[l].astype(f32))
    return dict(
        w_all=w_all, b_all=b_all, w_ab=w_ab,
        gdn_conv=gdn_conv[l].astype(f32), a_log=gdn_a_log[l].astype(f32), dt_bias=gdn_dt_bias[l].astype(f32),
        gdn_norm_w=gdn_norm_w[l].astype(f32)[None, :],
        bias_tiles=bias_tiles, lam=lam, diff_norm_w=diff_norm_w[l].astype(f32)[None, :],
        w_mem_kv=w_mem_kv[l].astype(bf16),
        w_bg=w_branch_gdn[l].astype(bf16), w_bd=w_branch_diff[l].astype(bf16), w_bc=w_branch_cross[l].astype(bf16),
        w_out=w_out[l].astype(bf16),
        ln1_g=ln1_g[l].astype(f32)[None, :], ln1_b=ln1_b[l].astype(f32)[None, :],
        w_up_r=interleave(w_up[l]).astype(bf16), ffn_conv_r=interleave(ffn_conv[l].astype(f32)),
        w_down_p=jnp.pad(w_down[l], ((0, padc), (0, 0))).astype(bf16),
        ln2_g=ln2_g[l].astype(f32)[None, :], ln2_b=ln2_b[l].astype(f32)[None, :],
    )


def kernel(x_prompt, x_sample, mem_prompt, mem_sample, rel_bias, w_in, gdn_conv, gdn_a_log, gdn_dt_bias, gdn_norm_w, diff_lambda, diff_norm_w, w_mem_kv, w_gate, b_gate, w_branch_gdn, w_branch_diff, w_branch_cross, w_out, ln1_g, ln1_b, w_up, ffn_conv, w_down, ln2_g, ln2_b):
    wts = _prep_weights(rel_bias, w_in, gdn_conv, gdn_a_log, gdn_dt_bias, gdn_norm_w, diff_lambda, diff_norm_w,
                        w_mem_kv, w_gate, b_gate, w_branch_gdn, w_branch_diff, w_branch_cross, w_out,
                        ln1_g, ln1_b, w_up, ffn_conv, w_down, ln2_g, ln2_b)
    return (_trunk(x_prompt, mem_prompt, wts), _trunk(x_sample, mem_sample, wts))
```

```python
import functools
import math

import jax
import jax.numpy as jnp
import numpy as np
from jax import lax
from jax.experimental import pallas as pl
from jax.experimental.pallas import tpu as pltpu

f32 = jnp.float32
bf16 = jnp.bfloat16

D_MODEL = 2048
GDN_HEADS = 8
GDN_HEAD_DIM = 128
GDN_WIDTH = 1024
GDN_CONV = 5
GDN_CHUNK = 64
DIFF_HEADS = 4
DIFF_HEAD_DIM = 128
CROSS_HEADS = 4
CROSS_HEAD_DIM = 256
N_BRANCH = 3
D_FF = 5504
NUM_BUCKETS = 32
MAX_DISTANCE = 128
LN_EPS = 1e-5
RMS_EPS = 1e-6
L2_EPS = 1e-6
DEPTH = 1
DEEPNORM_ALPHA = (2 * DEPTH) ** 0.25
LAMBDA_INIT = 0.8 - 0.6 * math.exp(-0.3 * 0)

LANES = 128
MXU_WIDTH = 256
VMEM_LIMIT_BYTES = 56 * 1024 * 1024

PROJ_LIN = 8192
COL_GQ, COL_GK, COL_GV, COL_GZ = 0, 1024, 2048, 3072
COL_DQ, COL_DK, COL_DV, COL_CQ = 4096, 5120, 6144, 7168
COL_GATE = PROJ_LIN
PROJ_COLS = PROJ_LIN + N_BRANCH * D_MODEL
D_FF_PAD = 5632

NT_DIMS = (((1,), (1,)), ((), ()))


def _cparams(sem, vmem=VMEM_LIMIT_BYTES):
    return pltpu.CompilerParams(dimension_semantics=sem, vmem_limit_bytes=vmem)


def _sigmoid(x):
    return 1.0 / (1.0 + jnp.exp(-x))


def _silu(x):
    return x * _sigmoid(x)


def _layer_norm(y, g, b):
    mu = jnp.mean(y, axis=-1, keepdims=True)
    yc = y - mu
    var = jnp.mean(yc * yc, axis=-1, keepdims=True)
    return yc * lax.rsqrt(var + LN_EPS) * g + b


def _bdot(a, b):
    return jnp.dot(a.astype(bf16), b.astype(bf16), preferred_element_type=f32)


def _inproj_kernel(x_ref, w_ref, b_ref, wab_ref, o_ref, ab_ref, xb_ref, *, n_lin_tiles):
    j = pl.program_id(1)

    @pl.when(j == 0)
    def _():
        xb = x_ref[...].astype(bf16)
        xb_ref[...] = xb
        ab_ref[...] = jnp.dot(xb, wab_ref[...], preferred_element_type=f32)

    acc = jnp.dot(xb_ref[...], w_ref[...], preferred_element_type=f32)
    o_ref[...] = jnp.where(j >= n_lin_tiles, _sigmoid(acc + b_ref[...]), acc).astype(bf16)


def _inproj(x2, w_all, b_all, w_ab, *, tm, tn):
    T = x2.shape[0]
    n_lin_tiles = PROJ_LIN // tn
    return pl.pallas_call(
        functools.partial(_inproj_kernel, n_lin_tiles=n_lin_tiles),
        out_shape=(jax.ShapeDtypeStruct((T, PROJ_COLS), bf16),
                   jax.ShapeDtypeStruct((T, LANES), f32)),
        grid=(T // tm, PROJ_COLS // tn),
        in_specs=[pl.BlockSpec((tm, D_MODEL), lambda i, j: (i, 0)),
                  pl.BlockSpec((D_MODEL, tn), lambda i, j: (0, j)),
                  pl.BlockSpec((1, tn), lambda i, j: (0, j)),
                  pl.BlockSpec((D_MODEL, LANES), lambda i, j: (0, 0))],
        out_specs=(pl.BlockSpec((tm, tn), lambda i, j: (i, j)),
                   pl.BlockSpec((tm, LANES), lambda i, j: (i, 0))),
        scratch_shapes=[pltpu.VMEM((tm, D_MODEL), bf16)],
        compiler_params=_cparams(("parallel", "arbitrary")),
        name="inproj",
    )(x2, w_all, b_all, w_ab)


def _mm_kernel(x_ref, w_ref, o_ref):
    o_ref[...] = jnp.dot(x_ref[...].astype(bf16), w_ref[...], preferred_element_type=f32).astype(o_ref.dtype)


def _matmul(x2, w, *, tm, out_dtype=bf16):
    T, K = x2.shape
    N = w.shape[1]
    return pl.pallas_call(
        _mm_kernel,
        out_shape=jax.ShapeDtypeStruct((T, N), out_dtype),
        grid=(T // tm,),
        in_specs=[pl.BlockSpec((tm, K), lambda i: (i, 0)),
                  pl.BlockSpec((K, N), lambda i: (0, 0), pipeline_mode=pl.Buffered(1))],
        out_specs=pl.BlockSpec((tm, N), lambda i: (i, 0)),
        compiler_params=_cparams(("parallel",)),
        name="mem_kv",
    )(x2, w)


SB = 2 * GDN_CHUNK
CONV_ROWS = 256
CONV_PAD = 8
GDN_UNROLL = 8
INV_BASE = 8


def _split_bf16(x):
    hi = x.astype(bf16)
    lo = (x - hi.astype(f32)).astype(bf16)
    return hi, lo


def _gdn_kernel(alog_ref, dtb_ref, q_ref, k_ref, v_ref, z_ref, ab_ref, cq_ref, ck_ref, cv_ref,
                nw_ref, o_ref, qn_ref, kn_ref, vn_ref, xp_ref, lhs_ref, n_ref, egl_ref, oacc_ref, *, S, unroll):
    h = pl.program_id(1)
    nsb = S // SB
    hd = GDN_HEAD_DIM

    zpad = jnp.zeros((CONV_PAD, hd), f32)
    streams = ((q_ref, cq_ref, qn_ref, "q"), (k_ref, ck_ref, kn_ref, "k"), (v_ref, cv_ref, vn_ref, "v"))
    for si, (src_ref, _, _, _) in enumerate(streams):
        xp_ref[si, 0:CONV_PAD, :] = zpad
        xp_ref[si, S + CONV_PAD:S + 2 * CONV_PAD, :] = zpad
        xp_ref[si, CONV_PAD:S + CONV_PAD, :] = src_ref[0].astype(f32)
    taps = [w_ref[...] for _, w_ref, _, _ in streams]

    def conv_body(r, c):
        r0 = pl.multiple_of(r * CONV_ROWS, CONV_ROWS)
        for si, (_, _, dst_ref, mode) in enumerate(streams):
            w = taps[si]
            acc = jnp.zeros((CONV_ROWS, hd), f32)
            for j in range(GDN_CONV):
                acc = acc + xp_ref[si, pl.ds(r0 + CONV_PAD - GDN_CONV // 2 + j, CONV_ROWS), :] * w[j:j + 1, :]
            y = _silu(acc)
            if mode != "v":
                y = y * lax.rsqrt(jnp.sum(y * y, axis=-1, keepdims=True) + L2_EPS)
            if mode == "q":
                y = y * (hd ** -0.5)
            dst_ref[pl.ds(r0, CONV_ROWS), :] = y
        return c

    lax.fori_loop(0, S // CONV_ROWS, conv_body, 0)

    ri = lax.broadcasted_iota(jnp.int32, (SB, SB), 0)
    ci = lax.broadcasted_iota(jnp.int32, (SB, SB), 1)

    def same(blk):
        return (ri // blk) == (ci // blk)

    same_chunk = same(GDN_CHUNK)
    incl = (same_chunk & (ri >= ci), same_chunk & (ri <= ci))
    strict = (same_chunk & (ri > ci), same_chunk & (ri < ci))
    same_base = same(INV_BASE)
    merge_masks = []
    blk = INV_BASE
    while blk < GDN_CHUNK:
        merge_masks.append(same(2 * blk) & jnp.logical_not(same(blk)))
        blk *= 2
    eye = (ri == ci).astype(f32)
    tri2 = [jnp.concatenate([incl[d].astype(bf16)] * 2, axis=1) for d in range(2)]
    r4 = lax.broadcasted_iota(jnp.int32, (2 * SB, 4 * hd), 0) % SB
    c4 = lax.broadcasted_iota(jnp.int32, (2 * SB, 4 * hd), 1) // hd
    sel = (r4 == (c4 * GDN_HEADS + h)).astype(bf16)
    rh = lax.broadcasted_iota(jnp.int32, (SB, 4 * hd), 0) < GDN_CHUNK
    ch = lax.broadcasted_iota(jnp.int32, (SB, 4 * hd), 1) < 2 * hd
    half_mask = rh == ch

    def phase_a(i, c):
        chains = []
        for u in range(unroll):
            sb = i * unroll + u
            rows = pl.ds(pl.multiple_of(sb * SB, SB), SB)
            kn = kn_ref[rows, :]
            qn = qn_ref[rows, :]
            vn = vn_ref[rows, :]
            knb = kn.astype(bf16)
            kq = lax.dot_general(jnp.concatenate([knb, qn.astype(bf16)], axis=0), knb, NT_DIMS,
                                 preferred_element_type=f32)
            ab_hi, ab_lo = _split_bf16(ab_ref[0, rows, :])
            absel = jnp.dot(jnp.concatenate([ab_hi, ab_lo], axis=1), sel, preferred_element_type=f32)
            for d in range(2):
                chains.append(dict(sb=sb, rows=rows, d=d, kn=kn, qn=qn, vn=vn, kk=kq[:SB], qk=kq[SB:],
                                   a=absel[:, d * hd:(d + 1) * hd], b=absel[:, (2 + d) * hd:(3 + d) * hd]))
        for t in chains:
            d = t["d"]
            xs = t["a"] + dtb_ref[d, h]
            softplus = jnp.maximum(xs, 0.0) + jnp.log1p(jnp.exp(-jnp.abs(xs)))
            g = -jnp.exp(jnp.full((1, hd), alog_ref[d, h], f32)) * softplus
            t["beta"] = _sigmoid(t["b"])
            g_hi, g_lo = _split_bf16(g)
            t["gc"] = jnp.dot(tri2[d], jnp.concatenate([g_hi, g_lo], axis=0),
                              preferred_element_type=f32)
        for t in chains:
            d, gc, beta = t["d"], t["gc"], t["beta"]
            diff = gc - gc.T
            t["decay"] = jnp.where(incl[d], jnp.exp(jnp.where(incl[d], diff, 0.0)), 0.0)
            t["l"] = jnp.where(strict[d], beta * t["kk"] * t["decay"], 0.0)
            t["egc"] = jnp.exp(gc)
            t["rhs"] = jnp.concatenate([beta * t["vn"], (beta * t["egc"]) * t["kn"]], axis=1)
            t["ld"] = jnp.where(same_base, t["l"], 0.0)
            t["ld2"] = _bdot(t["ld"], t["ld"])
        for t in chains:
            iml = eye - t["ld"]
            t["x"] = iml + _bdot(iml, t["ld2"])
            t["ld4"] = _bdot(t["ld2"], t["ld2"])
        for t in chains:
            t["t"] = t["x"] + _bdot(t["x"], t["ld4"])
        for mask in merge_masks:
            for t in chains:
                t["w"] = _bdot(t["t"], jnp.where(mask, t["l"], 0.0))
            for t in chains:
                t["t"] = t["t"] - _bdot(t["w"], t["t"])
        for t in chains:
            t["sol"] = _bdot(t["t"], t["rhs"])
        for t in chains:
            d, gc, sol = t["d"], t["gc"], t["sol"]
            w = jnp.concatenate([sol[:, hd:], sol[:, :hd]], axis=1).astype(bf16)
            qkd = jnp.where(incl[d], t["qk"] * t["decay"], 0.0).astype(bf16)
            t["qw"] = jnp.dot(qkd, w, preferred_element_type=f32)
            last = (GDN_CHUNK - 1) if d == 0 else 0
            gl = jnp.concatenate(
                [jnp.broadcast_to(gc[last:last + 1, :], (GDN_CHUNK, hd)),
                 jnp.broadcast_to(gc[GDN_CHUNK + last:GDN_CHUNK + last + 1, :], (GDN_CHUNK, hd))], axis=0)
            kdt = (t["kn"] * jnp.exp(gl - gc)).T.astype(bf16)
            wpair = jnp.where(half_mask, jnp.concatenate([w, w], axis=1), jnp.zeros((SB, 4 * hd), bf16))
            t["kw"] = jnp.dot(kdt, wpair, preferred_element_type=f32)
            t["egl"] = jnp.exp(gl)
        for t in chains:
            d = t["d"]
            f_, s_ = (0, 1) if d == 0 else (1, 0)
            kw = t["kw"]
            t["qp"] = t["qn"] * t["egc"] - t["qw"][:, :hd]
            a = [-kw[:, 2 * hf * hd:(2 * hf + 1) * hd] for hf in range(2)]
            nn = [kw[:, (2 * hf + 1) * hd:(2 * hf + 2) * hd] for hf in range(2)]
            qs = t["qp"][s_ * GDN_CHUNK:(s_ + 1) * GDN_CHUNK, :]
            t["am"], t["nm"] = a, nn
            t["xc"] = _bdot(jnp.concatenate([a[s_], qs], axis=0), jnp.concatenate([a[f_], nn[f_]], axis=1))
        for t in chains:
            d, sb = t["d"], t["sb"]
            f_, s_ = (0, 1) if d == 0 else (1, 0)
            a, nn, x, qp = t["am"], t["nm"], t["xc"], t["qp"]
            dec = [t["egl"][hf * GDN_CHUNK:hf * GDN_CHUNK + 1, :] for hf in range(2)]
            a2 = dec[s_] * a[f_] + dec[f_] * a[s_] + x[:hd, :hd]
            n2 = dec[s_] * nn[f_] + x[:hd, hd:] + nn[s_]
            q_f = qp[f_ * GDN_CHUNK:(f_ + 1) * GDN_CHUNK, :]
            q_s = dec[f_] * qp[s_ * GDN_CHUNK:(s_ + 1) * GDN_CHUNK, :] + x[hd:, :hd]
            q2 = jnp.concatenate([q_f, q_s] if d == 0 else [q_s, q_f], axis=0)
            zero = jnp.zeros((GDN_CHUNK, hd), f32)
            t["o"] = t["qw"][:, hd:] + jnp.concatenate([zero, x[hd:, hd:]] if d == 0 else [x[hd:, hd:], zero], axis=0)
            l0 = pl.multiple_of(sb * 2 * SB, 2 * SB)
            lhs_ref[d, pl.ds(l0, SB), :] = a2.astype(bf16)
            lhs_ref[d, pl.ds(l0 + SB, SB), :] = q2.astype(bf16)
            n_ref[d, pl.ds(pl.multiple_of(sb * SB, SB), SB), :] = n2
            egl_ref[d, pl.ds(pl.multiple_of(sb * 8, 8), 8), :] = jnp.broadcast_to(dec[0] * dec[1], (8, hd))
        for tf, tb in zip(chains[0::2], chains[1::2]):
            oacc_ref[tf["rows"], :] = tf["o"] + tb["o"]
        return c

    lax.fori_loop(0, nsb // unroll, phase_a, 0)

    def sb_step(d, sbi, state):
        lhs = lhs_ref[d, pl.ds(pl.multiple_of(sbi * 2 * SB, 2 * SB), 2 * SB), :]
        pop = jnp.dot(lhs, state.astype(bf16), preferred_element_type=f32)
        rows = pl.ds(pl.multiple_of(sbi * SB, SB), SB)
        oacc_ref[rows, :] += pop[SB:, :]
        dec = egl_ref[d, pl.ds(pl.multiple_of(sbi * 8, 8), 8), :][0:1, :]
        return dec * state + pop[:SB, :] + n_ref[d, rows, :]

    def phase_b(t, carry):
        sf, sbk = carry
        return sb_step(0, t, sf), sb_step(1, nsb - 1 - t, sbk)

    z0 = jnp.zeros((hd, hd), f32)
    lax.fori_loop(0, nsb, phase_b, (z0, z0))

    nw = nw_ref[...]

    def fin(r, c):
        rows = pl.ds(pl.multiple_of(r * CONV_ROWS, CONV_ROWS), CONV_ROWS)
        o = oacc_ref[rows, :]
        o = o * lax.rsqrt(jnp.mean(o * o, axis=-1, keepdims=True) + RMS_EPS) * nw
        o_ref[0, rows, :] = (o * _silu(z_ref[0, rows, :].astype(f32))).astype(bf16)
        return c

    lax.fori_loop(0, S // CONV_ROWS, fin, 0)


def _gdn(proj3, ab3, conv_w, a_log, dt_bias, norm_w):
    B, S, _ = proj3.shape
    hb = GDN_HEAD_DIM
    nsb = S // SB

    def col(base):
        return pl.BlockSpec((1, S, hb), lambda b, h, base=base: (b, 0, base // hb + h))

    def cw(base):
        return pl.BlockSpec((GDN_CONV, hb), lambda b, h, base=base: (0, base // hb + h))

    smem = pl.BlockSpec(memory_space=pltpu.SMEM)
    return pl.pallas_call(
        functools.partial(_gdn_kernel, S=S, unroll=GDN_UNROLL),
        out_shape=jax.ShapeDtypeStruct((B, S, GDN_WIDTH), bf16),
        grid=(B, GDN_HEADS),
        in_specs=[smem, smem, col(COL_GQ), col(COL_GK), col(COL_GV), col(COL_GZ),
                  pl.BlockSpec((1, S, LANES), lambda b, h: (b, 0, 0)),
                  cw(0), cw(GDN_WIDTH), cw(2 * GDN_WIDTH),
                  pl.BlockSpec((1, hb), lambda b, h: (0, 0))],
        out_specs=pl.BlockSpec((1, S, hb), lambda b, h: (b, 0, h)),
        scratch_shapes=[pltpu.VMEM((S, hb), f32), pltpu.VMEM((S, hb), f32), pltpu.VMEM((S, hb), f32),
                        pltpu.VMEM((3, S + 2 * CONV_PAD, hb), f32),
                        pltpu.VMEM((2, nsb * 2 * SB, hb), bf16),
                        pltpu.VMEM((2, nsb * SB, hb), f32),
                        pltpu.VMEM((2, nsb * 8, hb), f32),
                        pltpu.VMEM((S, hb), f32)],
        compiler_params=_cparams(("parallel", "arbitrary")),
        name="gdn",
    )(a_log, dt_bias, proj3, proj3, proj3, proj3, ab3, conv_w, conv_w, conv_w, norm_w)


N_BIAS_TILES = 6
KV_TILE = 2 * LANES


def _bias_kernel(rb_ref, bucket_ref, lp_ref, bias_ref, lam_ref):
    bucket = bucket_ref[...]
    for hh in range(DIFF_HEADS):
        acc = jnp.zeros(bucket.shape, f32)
        for bk in range(NUM_BUCKETS):
            acc = jnp.where(bucket == bk, rb_ref[bk, hh], acc)
        bias_ref[hh] = acc
    lp = lp_ref[...]
    lam = (jnp.exp(jnp.sum(lp[0:1] * lp[1:2], axis=-1, keepdims=True))
           - jnp.exp(jnp.sum(lp[2:3] * lp[3:4], axis=-1, keepdims=True)) + LAMBDA_INIT)
    lam_ref[...] = jnp.broadcast_to(lam, lam_ref.shape)


def _t5_bucket(rel):
    nb = NUM_BUCKETS // 2
    max_exact = nb // 2
    ret = jnp.where(rel > 0, nb, 0)
    n = jnp.abs(rel)
    nf = jnp.maximum(n, 1).astype(jnp.float32)
    large = max_exact + (jnp.log(nf / max_exact) / math.log(MAX_DISTANCE / max_exact)
                         * (nb - max_exact)).astype(jnp.int32)
    large = jnp.minimum(large, nb - 1)
    return ret + jnp.where(n < max_exact, n, large)


def _bias_tiles(rel_bias, lam_params):
    delta = jnp.arange(N_BIAS_TILES, dtype=jnp.int32)[:, None, None] - 3
    rel = (LANES * delta + jnp.arange(KV_TILE, dtype=jnp.int32)[None, None, :]
           - jnp.arange(LANES, dtype=jnp.int32)[None, :, None])
    bucket = _t5_bucket(rel).astype(jnp.int32)
    return pl.pallas_call(
        _bias_kernel,
        out_shape=(jax.ShapeDtypeStruct((DIFF_HEADS, N_BIAS_TILES, LANES, KV_TILE), f32),
                   jax.ShapeDtypeStruct((8, LANES), f32)),
        in_specs=[pl.BlockSpec(memory_space=pltpu.SMEM),
                  pl.BlockSpec(memory_space=pltpu.VMEM),
                  pl.BlockSpec(memory_space=pltpu.VMEM)],
        out_specs=(pl.BlockSpec(memory_space=pltpu.VMEM), pl.BlockSpec(memory_space=pltpu.VMEM)),
        name="rel_bias",
    )(rel_bias, bucket, lam_params)


def _diff_kernel(q_ref, k_ref, v_ref, bias_ref, lam_ref, nw_ref, o_ref, s_ref, e_ref, *, S, QB):
    qi = pl.program_id(2)
    n_tiles = S // KV_TILE
    nsub = QB // LANES
    dh = DIFF_HEAD_DIM
    q = (q_ref[0].astype(f32) * (dh ** -0.5)).astype(bf16)

    def tile(jt):
        return slice(jt * KV_TILE, (jt + 1) * KV_TILE)

    def score_tile(m, jt, mx):
        s = lax.dot_general(q[:, m * dh:(m + 1) * dh], k_ref[0, tile(jt), m * dh:(m + 1) * dh], NT_DIMS,
                            preferred_element_type=f32)
        parts = []
        for r in range(nsub):
            idx = jnp.clip(2 * jt - (qi * nsub + r), -3, 2) + 3
            parts.append(s[r * LANES:(r + 1) * LANES, :] + bias_ref[0, idx])
        s = jnp.concatenate(parts, axis=0) if nsub > 1 else parts[0]
        s_ref[m, :, tile(jt)] = s
        return jnp.maximum(mx, jnp.maximum(s[:, :LANES], s[:, LANES:]))

    def exp_tile(m, jt, row_max, den):
        e = jnp.exp(s_ref[m, :, tile(jt)] - row_max)
        e_ref[m, :, tile(jt)] = e.astype(bf16)
        return den + (e[:, :LANES] + e[:, LANES:])

    def pv_tile(m, jt, acc):
        return acc + jnp.dot(e_ref[m, :, tile(jt)], v_ref[0, tile(jt), :], preferred_element_type=f32)

    neg = jnp.full((QB, LANES), -jnp.inf, f32)
    zero = jnp.zeros((QB, LANES), f32)
    mx0 = neg
    for jt in range(n_tiles):
        mx0 = score_tile(0, jt, mx0)
    max0 = jnp.max(mx0, axis=-1, keepdims=True)
    mx1, den0 = neg, zero
    for jt in range(n_tiles):
        mx1 = score_tile(1, jt, mx1)
        den0 = exp_tile(0, jt, max0, den0)
    max1 = jnp.max(mx1, axis=-1, keepdims=True)
    pv0, den1 = jnp.zeros((QB, 2 * dh), f32), zero
    for jt in range(n_tiles):
        pv0 = pv_tile(0, jt, pv0)
        den1 = exp_tile(1, jt, max1, den1)
    pv1 = jnp.zeros((QB, 2 * dh), f32)
    for jt in range(n_tiles):
        pv1 = pv_tile(1, jt, pv1)
    outs = [pv0 / jnp.sum(den0, axis=-1, keepdims=True), pv1 / jnp.sum(den1, axis=-1, keepdims=True)]
    lam = lam_ref[0:1, 0:1]
    o = outs[0] - lam * outs[1]
    o = o * lax.rsqrt(jnp.mean(o * o, axis=-1, keepdims=True) + RMS_EPS) * nw_ref[...]
    o_ref[0] = (o * (1.0 - LAMBDA_INIT)).astype(bf16)


def _diff_attention(proj3, bias_tiles, lam, norm_w, *, qb):
    B, S, _ = proj3.shape
    w = 2 * DIFF_HEAD_DIM
    return pl.pallas_call(
        functools.partial(_diff_kernel, S=S, QB=qb),
        out_shape=jax.ShapeDtypeStruct((B, S, DIFF_HEADS * w), bf16),
        grid=(B, DIFF_HEADS, S // qb),
        in_specs=[pl.BlockSpec((1, qb, w), lambda b, h, i: (b, i, COL_DQ // w + h)),
                  pl.BlockSpec((1, S, w), lambda b, h, i: (b, 0, COL_DK // w + h)),
                  pl.BlockSpec((1, S, w), lambda b, h, i: (b, 0, COL_DV // w + h)),
                  pl.BlockSpec((1, N_BIAS_TILES, LANES, KV_TILE), lambda b, h, i: (h, 0, 0, 0)),
                  pl.BlockSpec((8, LANES), lambda b, h, i: (0, 0)),
                  pl.BlockSpec((1, w), lambda b, h, i: (0, 0))],
        out_specs=pl.BlockSpec((1, qb, w), lambda b, h, i: (b, i, h)),
        scratch_shapes=[pltpu.VMEM((2, qb, S), f32), pltpu.VMEM((2, qb, S), bf16)],
        compiler_params=_cparams(("parallel", "parallel", "arbitrary")),
        name="diff_attn",
    )(proj3, proj3, proj3, bias_tiles, lam, norm_w)


def _cross_kernel(q_ref, k_ref, v_ref, o_ref):
    s = lax.dot_general(q_ref[0], k_ref[0], NT_DIMS, preferred_element_type=f32) * (CROSS_HEAD_DIM ** -0.5)
    e = jnp.exp(s - jnp.max(s, axis=-1, keepdims=True))
    p = e / jnp.sum(e, axis=-1, keepdims=True)
    o_ref[0] = jnp.dot(p.astype(bf16), v_ref[0], preferred_element_type=f32).astype(bf16)


def _cross_attention(proj3, kv3, *, qb):
    B, S, _ = proj3.shape
    M = kv3.shape[1]
    w = CROSS_HEAD_DIM
    return pl.pallas_call(
        _cross_kernel,
        out_shape=jax.ShapeDtypeStruct((B, S, CROSS_HEADS * w), bf16),
        grid=(B, CROSS_HEADS, S // qb),
        in_specs=[pl.BlockSpec((1, qb, w), lambda b, h, i: (b, i, COL_CQ // w + h)),
                  pl.BlockSpec((1, M, w), lambda b, h, i: (b, 0, h)),
                  pl.BlockSpec((1, M, w), lambda b, h, i: (b, 0, CROSS_HEADS + h))],
        out_specs=pl.BlockSpec((1, qb, w), lambda b, h, i: (b, i, h)),
        compiler_params=_cparams(("parallel", "parallel", "arbitrary")),
        name="cross_attn",
    )(proj3, kv3, kv3)


def _merge_kernel(yg_ref, yd_ref, yc_ref, g0_ref, g1_ref, g2_ref, wg_ref, wd_ref, wc_ref, o_ref):
    m = g0_ref[...].astype(f32) * jnp.dot(yg_ref[...], wg_ref[...], preferred_element_type=f32)
    m = m + g1_ref[...].astype(f32) * jnp.dot(yd_ref[...], wd_ref[...], preferred_element_type=f32)
    m = m + g2_ref[...].astype(f32) * jnp.dot(yc_ref[...], wc_ref[...], preferred_element_type=f32)
    o_ref[...] = m.astype(bf16)


def _merge(yg, yd, yc, proj2, wg, wd, wc, *, tm):
    T = yg.shape[0]
    gate_blk = COL_GATE // D_MODEL

    def yspec():
        return pl.BlockSpec((tm, GDN_WIDTH), lambda i: (i, 0))

    def gspec(n):
        return pl.BlockSpec((tm, D_MODEL), lambda i, n=n: (i, gate_blk + n))

    def wspec():
        return pl.BlockSpec((GDN_WIDTH, D_MODEL), lambda i: (0, 0), pipeline_mode=pl.Buffered(1))

    return pl.pallas_call(
        _merge_kernel,
        out_shape=jax.ShapeDtypeStruct((T, D_MODEL), bf16),
        grid=(T // tm,),
        in_specs=[yspec(), yspec(), yspec(), gspec(0), gspec(1), gspec(2), wspec(), wspec(), wspec()],
        out_specs=pl.BlockSpec((tm, D_MODEL), lambda i: (i, 0)),
        compiler_params=_cparams(("parallel",)),
        name="merge",
    )(yg, yd, yc, proj2, proj2, proj2, wg, wd, wc)


def _outproj_kernel(m_ref, x_ref, w_ref, g_ref, b_ref, o_ref):
    y = DEEPNORM_ALPHA * x_ref[...] + jnp.dot(m_ref[...], w_ref[...], preferred_element_type=f32)
    o_ref[...] = _layer_norm(y, g_ref[...], b_ref[...])


def _outproj_ln(merged, x2, w_out, g, b, *, tm):
    T = x2.shape[0]
    return pl.pallas_call(
        _outproj_kernel,
        out_shape=jax.ShapeDtypeStruct((T, D_MODEL), f32),
        grid=(T // tm,),
        in_specs=[pl.BlockSpec((tm, D_MODEL), lambda i: (i, 0)),
                  pl.BlockSpec((tm, D_MODEL), lambda i: (i, 0)),
                  pl.BlockSpec((D_MODEL, D_MODEL), lambda i: (0, 0), pipeline_mode=pl.Buffered(1)),
                  pl.BlockSpec((1, D_MODEL), lambda i: (0, 0)),
                  pl.BlockSpec((1, D_MODEL), lambda i: (0, 0))],
        out_specs=pl.BlockSpec((tm, D_MODEL), lambda i: (i, 0)),
        compiler_params=_cparams(("parallel",)),
        name="outproj_ln",
    )(merged, x2, w_out, g, b)


HALO = 16
FFN_TN = 512
FFN_SUB = 128
FFN_ROWS = 128


def _ffn_kernel(xm_ref, xp_ref, xn_ref, wup_ref, cw_ref, wdn_ref, g_ref, b_ref, o_ref, xh_ref, up_ref, h_ref,
                *, tm, tn, tiles_per_seq):
    i = pl.program_id(0)
    n = pl.program_id(1)

    @pl.when(n == 0)
    def _():
        pos = i % tiles_per_seq
        keep_prev = (pos != 0).astype(f32)
        keep_next = (pos != tiles_per_seq - 1).astype(f32)
        xh_ref[0:HALO, :] = (xp_ref[...] * keep_prev).astype(bf16)
        xh_ref[HALO:HALO + tm, :] = xm_ref[...].astype(bf16)
        xh_ref[HALO + tm:2 * HALO + tm, :] = (xn_ref[...] * keep_next).astype(bf16)
        o_ref[...] = DEEPNORM_ALPHA * xm_ref[...]

    cw = cw_ref[...]
    sub = 2 * FFN_SUB
    rows = tm + 2 * HALO

    def up_dot(c):
        up = jnp.dot(xh_ref[...], wup_ref[:, c * sub:(c + 1) * sub], preferred_element_type=f32)
        up_ref[c % 2, 0] = up[:, :FFN_SUB]
        up_ref[c % 2, 1] = up[:, FFN_SUB:]

    def act(c):
        w = cw[:, c * sub:(c + 1) * sub]
        for r in range(tm // FFN_ROWS):
            r0 = r * FFN_ROWS
            conv = []
            for p in range(2):
                wp = w[:, p * FFN_SUB:(p + 1) * FFN_SUB]
                conv.append(sum(up_ref[c % 2, p, r0 + HALO - 1 + j:r0 + HALO - 1 + j + FFN_ROWS, :] * wp[j:j + 1, :]
                                for j in range(3)))
            h_ref[r0:r0 + FFN_ROWS, c * FFN_SUB:(c + 1) * FFN_SUB] = (_silu(conv[0]) * conv[1]).astype(bf16)

    nsub = tn // FFN_SUB
    up_dot(0)
    for c in range(nsub):
        if c + 1 < nsub:
            up_dot(c + 1)
        act(c)
    o_ref[...] += jnp.dot(h_ref[...], wdn_ref[...], preferred_element_type=f32)

    @pl.when(n == pl.num_programs(1) - 1)
    def _():
        o_ref[...] = _layer_norm(o_ref[...], g_ref[...], b_ref[...])


def _ffn_ln(x1, w_up_r, conv_r, w_down_p, g, b, *, S, tm, tn):
    T = x1.shape[0]
    nf = D_FF_PAD // tn
    hpt = tm // HALO
    last_halo = T // HALO - 1
    return pl.pallas_call(
        functools.partial(_ffn_kernel, tm=tm, tn=tn, tiles_per_seq=S // tm),
        out_shape=jax.ShapeDtypeStruct((T, D_MODEL), f32),
        grid=(T // tm, nf),
        in_specs=[pl.BlockSpec((tm, D_MODEL), lambda i, n: (i, 0)),
                  pl.BlockSpec((HALO, D_MODEL), lambda i, n: (jnp.maximum(i * hpt - 1, 0), 0)),
                  pl.BlockSpec((HALO, D_MODEL), lambda i, n: (jnp.minimum((i + 1) * hpt, last_halo), 0)),
                  pl.BlockSpec((D_MODEL, 2 * tn), lambda i, n: (0, n)),
                  pl.BlockSpec((3, 2 * tn), lambda i, n: (0, n)),
                  pl.BlockSpec((tn, D_MODEL), lambda i, n: (n, 0)),
                  pl.BlockSpec((1, D_MODEL), lambda i, n: (0, 0)),
                  pl.BlockSpec((1, D_MODEL), lambda i, n: (0, 0))],
        out_specs=pl.BlockSpec((tm, D_MODEL), lambda i, n: (i, 0)),
        scratch_shapes=[pltpu.VMEM((tm + 2 * HALO, D_MODEL), bf16),
                        pltpu.VMEM((2, 2, tm + 2 * HALO, FFN_SUB), f32),
                        pltpu.VMEM((tm, tn), bf16)],
        compiler_params=_cparams(("parallel", "arbitrary")),
        name="ffn_ln",
    )(x1, x1, x1, w_up_r, conv_r, w_down_p, g, b)


def _pick(n, prefs):
    for p in prefs:
        if n % p == 0:
            return p
    raise ValueError(f"no tile in {prefs} divides {n}")


def _trunk(x, mem, wts):
    B, S, _ = x.shape
    T = B * S
    assert S % CONV_ROWS == 0 and S % KV_TILE == 0 and (S // SB) % GDN_UNROLL == 0
    x2 = x.reshape(T, D_MODEL)

    proj2, ab2 = _inproj(x2, wts["w_all"], wts["b_all"], wts["w_ab"], tm=_pick(T, (1024, 512, 256)), tn=1024)
    proj3 = proj2.reshape(B, S, PROJ_COLS)
    ab3 = ab2.reshape(B, S, LANES)

    y_gdn = _gdn(proj3, ab3, wts["gdn_conv"], wts["a_log"], wts["dt_bias"], wts["gdn_norm_w"])
    y_diff = _diff_attention(proj3, wts["bias_tiles"], wts["lam"], wts["diff_norm_w"], qb=_pick(S, (256,)))
    M = mem.shape[1]
    kv = _matmul(mem.reshape(B * M, D_MODEL), wts["w_mem_kv"], tm=_pick(B * M, (512, 256)))
    y_cross = _cross_attention(proj3, kv.reshape(B, M, 2 * CROSS_HEADS * CROSS_HEAD_DIM), qb=_pick(S, (512, 256)))

    tm = _pick(S, (512, 256))
    merged = _merge(y_gdn.reshape(T, -1), y_diff.reshape(T, -1), y_cross.reshape(T, -1), proj2,
                    wts["w_bg"], wts["w_bd"], wts["w_bc"], tm=tm)
    x1 = _outproj_ln(merged, x2, wts["w_out"], wts["ln1_g"], wts["ln1_b"], tm=tm)
    y = _ffn_ln(x1, wts["w_up_r"], wts["ffn_conv_r"], wts["w_down_p"], wts["ln2_g"], wts["ln2_b"],
                S=S, tm=tm, tn=FFN_TN)
    return y.reshape(B, S, D_MODEL)


def _prep_weights(rel_bias, w_in, gdn_conv, gdn_a_log, gdn_dt_bias, gdn_norm_w, diff_lambda, diff_norm_w,
                  w_mem_kv, w_gate, b_gate, w_branch_gdn, w_branch_diff, w_branch_cross, w_out,
                  ln1_g, ln1_b, w_up, ffn_conv, w_down, ln2_g, ln2_b):
    l = 0
    wi = w_in[l]
    a0 = 4 * GDN_WIDTH
    a1 = a0 + 4 * GDN_HEADS
    w_lin = jnp.concatenate([wi[:, :a0], wi[:, a1:]], axis=1)
    w_all = jnp.concatenate([w_lin, w_gate[l]], axis=1).astype(bf16)
    b_all = jnp.concatenate([jnp.zeros((PROJ_LIN,), f32), b_gate[l].astype(f32)])[None, :]
    w_ab = jnp.pad(wi[:, a0:a1], ((0, 0), (0, LANES - 4 * GDN_HEADS))).astype(bf16)

    nsub = D_FF_PAD // FFN_SUB
    padc = D_FF_PAD - D_FF

    def interleave(a):
        gate = jnp.pad(a[:, :D_FF], ((0, 0), (0, padc))).reshape(a.shape[0], nsub, FFN_SUB)
        val = jnp.pad(a[:, D_FF:], ((0, 0), (0, padc))).reshape(a.shape[0], nsub, FFN_SUB)
        return jnp.stack([gate, val], axis=2).reshape(a.shape[0], nsub * 2 * FFN_SUB)

    bias_tiles, lam = _bias_tiles(rel_bias.astype(f32), diff_lambda[l].astype(f32))
    return dict(
        w_all=w_all, b_all=b_all, w_ab=w_ab,
        gdn_conv=gdn_conv[l].astype(f32), a_log=gdn_a_log[l].astype(f32), dt_bias=gdn_dt_bias[l].astype(f32),
        gdn_norm_w=gdn_norm_w[l].astype(f32)[None, :],
        bias_tiles=bias_tiles, lam=lam, diff_norm_w=diff_norm_w[l].astype(f32)[None, :],
        w_mem_kv=w_mem_kv[l].astype(bf16),
        w_bg=w_branch_gdn[l].astype(bf16), w_bd=w_branch_diff[l].astype(bf16), w_bc=w_branch_cross[l].astype(bf16),
        w_out=w_out[l].astype(bf16),
        ln1_g=ln1_g[l].astype(f32)[None, :], ln1_b=ln1_b[l].astype(f32)[None, :],
        w_up_r=interleave(w_up[l]).astype(bf16), ffn_conv_r=interleave(ffn_conv[l].astype(f32)),
        w_down_p=jnp.pad(w_down[l], ((0, padc), (0, 0))).astype(bf16),
        ln2_g=ln2_g[l].astype(f32)[None, :], ln2_b=ln2_b[l].astype(f32)[None, :],
    )


def kernel(x_prompt, x_sample, mem_prompt, mem_sample, rel_bias, w_in, gdn_conv, gdn_a_log, gdn_dt_bias, gdn_norm_w, diff_lambda, diff_norm_w, w_mem_kv, w_gate, b_gate, w_branch_gdn, w_branch_diff, w_branch_cross, w_out, ln1_g, ln1_b, w_up, ffn_conv, w_down, ln2_g, ln2_b):
    wts = _prep_weights(rel_bias, w_in, gdn_conv, gdn_a_log, gdn_dt_bias, gdn_norm_w, diff_lambda, diff_norm_w,
                        w_mem_kv, w_gate, b_gate, w_branch_gdn, w_branch_diff, w_branch_cross, w_out,
                        ln1_g, ln1_b, w_up, ffn_conv, w_down, ln2_g, ln2_b)
    return (_trunk(x_prompt, mem_prompt, wts), _trunk(x_sample, mem_sample, wts))
```

```python
import functools
import math

import jax
import jax.numpy as jnp
import numpy as np
from jax import lax
from jax.experimental import pallas as pl
from jax.experimental.pallas import tpu as pltpu

f32 = jnp.float32
bf16 = jnp.bfloat16

D_MODEL = 2048
GDN_HEADS = 8
GDN_HEAD_DIM = 128
GDN_WIDTH = 1024
GDN_CONV = 5
GDN_CHUNK = 64
DIFF_HEADS = 4
DIFF_HEAD_DIM = 128
CROSS_HEADS = 4
CROSS_HEAD_DIM = 256
N_BRANCH = 3
D_FF = 5504
NUM_BUCKETS = 32
MAX_DISTANCE = 128
LN_EPS = 1e-5
RMS_EPS = 1e-6
L2_EPS = 1e-6
DEPTH = 1
DEEPNORM_ALPHA = (2 * DEPTH) ** 0.25
LAMBDA_INIT = 0.8 - 0.6 * math.exp(-0.3 * 0)

LANES = 128
MXU_WIDTH = 256
VMEM_LIMIT_BYTES = 56 * 1024 * 1024

PROJ_LIN = 8192
COL_GQ, COL_GK, COL_GV, COL_GZ = 0, 1024, 2048, 3072
COL_DQ, COL_DK, COL_DV, COL_CQ = 4096, 5120, 6144, 7168
COL_GATE = PROJ_LIN
PROJ_COLS = PROJ_LIN + N_BRANCH * D_MODEL
D_FF_PAD = 5632

NT_DIMS = (((1,), (1,)), ((), ()))


def _cparams(sem, vmem=VMEM_LIMIT_BYTES):
    return pltpu.CompilerParams(dimension_semantics=sem, vmem_limit_bytes=vmem)


def _sigmoid(x):
    return 1.0 / (1.0 + jnp.exp(-x))


def _silu(x):
    return x * _sigmoid(x)


def _layer_norm(y, g, b):
    mu = jnp.mean(y, axis=-1, keepdims=True)
    yc = y - mu
    var = jnp.mean(yc * yc, axis=-1, keepdims=True)
    return yc * lax.rsqrt(var + LN_EPS) * g + b


def _bdot(a, b):
    return jnp.dot(a.astype(bf16), b.astype(bf16), preferred_element_type=f32)


def _inproj_kernel(x_ref, w_ref, b_ref, wab_ref, o_ref, ab_ref, xb_ref, *, n_lin_tiles):
    j = pl.program_id(1)

    @pl.when(j == 0)
    def _():
        xb = x_ref[...].astype(bf16)
        xb_ref[...] = xb
        ab_ref[...] = jnp.dot(xb, wab_ref[...], preferred_element_type=f32)

    acc = jnp.dot(xb_ref[...], w_ref[...], preferred_element_type=f32)
    o_ref[...] = jnp.where(j >= n_lin_tiles, _sigmoid(acc + b_ref[...]), acc).astype(bf16)


def _inproj(x2, w_all, b_all, w_ab, *, tm, tn):
    T = x2.shape[0]
    n_lin_tiles = PROJ_LIN // tn
    return pl.pallas_call(
        functools.partial(_inproj_kernel, n_lin_tiles=n_lin_tiles),
        out_shape=(jax.ShapeDtypeStruct((T, PROJ_COLS), bf16),
                   jax.ShapeDtypeStruct((T, LANES), f32)),
        grid=(T // tm, PROJ_COLS // tn),
        in_specs=[pl.BlockSpec((tm, D_MODEL), lambda i, j: (i, 0)),
                  pl.BlockSpec((D_MODEL, tn), lambda i, j: (0, j)),
                  pl.BlockSpec((1, tn), lambda i, j: (0, j)),
                  pl.BlockSpec((D_MODEL, LANES), lambda i, j: (0, 0))],
        out_specs=(pl.BlockSpec((tm, tn), lambda i, j: (i, j)),
                   pl.BlockSpec((tm, LANES), lambda i, j: (i, 0))),
        scratch_shapes=[pltpu.VMEM((tm, D_MODEL), bf16)],
        compiler_params=_cparams(("parallel", "arbitrary")),
        name="inproj",
    )(x2, w_all, b_all, w_ab)


def _mm_kernel(x_ref, w_ref, o_ref):
    o_ref[...] = jnp.dot(x_ref[...].astype(bf16), w_ref[...], preferred_element_type=f32).astype(o_ref.dtype)


def _matmul(x2, w, *, tm, out_dtype=bf16):
    T, K = x2.shape
    N = w.shape[1]
    return pl.pallas_call(
        _mm_kernel,
        out_shape=jax.ShapeDtypeStruct((T, N), out_dtype),
        grid=(T // tm,),
        in_specs=[pl.BlockSpec((tm, K), lambda i: (i, 0)),
                  pl.BlockSpec((K, N), lambda i: (0, 0), pipeline_mode=pl.Buffered(1))],
        out_specs=pl.BlockSpec((tm, N), lambda i: (i, 0)),
        compiler_params=_cparams(("parallel",)),
        name="mem_kv",
    )(x2, w)


SB = 2 * GDN_CHUNK
CONV_ROWS = 256
CONV_PAD = 8
GDN_UNROLL = 8
INV_BASE = 8


def _split_bf16(x):
    hi = x.astype(bf16)
    lo = (x - hi.astype(f32)).astype(bf16)
    return hi, lo


def _gdn_kernel(alog_ref, dtb_ref, q_ref, k_ref, v_ref, z_ref, ab_ref, cq_ref, ck_ref, cv_ref,
                nw_ref, o_ref, qn_ref, kn_ref, vn_ref, xp_ref, lhs_ref, n_ref, egl_ref, oacc_ref, *, S, unroll):
    h = pl.program_id(1)
    nsb = S // SB
    hd = GDN_HEAD_DIM

    zpad = jnp.zeros((CONV_PAD, hd), f32)
    streams = ((q_ref, cq_ref, qn_ref, "q"), (k_ref, ck_ref, kn_ref, "k"), (v_ref, cv_ref, vn_ref, "v"))
    for si, (src_ref, _, _, _) in enumerate(streams):
        xp_ref[si, 0:CONV_PAD, :] = zpad
        xp_ref[si, S + CONV_PAD:S + 2 * CONV_PAD, :] = zpad
        xp_ref[si, CONV_PAD:S + CONV_PAD, :] = src_ref[0].astype(f32)
    taps = [w_ref[...] for _, w_ref, _, _ in streams]

    def conv_body(r, c):
        r0 = pl.multiple_of(r * CONV_ROWS, CONV_ROWS)
        for si, (_, _, dst_ref, mode) in enumerate(streams):
            w = taps[si]
            acc = jnp.zeros((CONV_ROWS, hd), f32)
            for j in range(GDN_CONV):
                acc = acc + xp_ref[si, pl.ds(r0 + CONV_PAD - GDN_CONV // 2 + j, CONV_ROWS), :] * w[j:j + 1, :]
            y = _silu(acc)
            if mode != "v":
                y = y * lax.rsqrt(jnp.sum(y * y, axis=-1, keepdims=True) + L2_EPS)
            if mode == "q":
                y = y * (hd ** -0.5)
            dst_ref[pl.ds(r0, CONV_ROWS), :] = y
        return c

    lax.fori_loop(0, S // CONV_ROWS, conv_body, 0)

    ri = lax.broadcasted_iota(jnp.int32, (SB, SB), 0)
    ci = lax.broadcasted_iota(jnp.int32, (SB, SB), 1)

    def same(blk):
        return (ri // blk) == (ci // blk)

    same_chunk = same(GDN_CHUNK)
    incl = (same_chunk & (ri >= ci), same_chunk & (ri <= ci))
    strict = (same_chunk & (ri > ci), same_chunk & (ri < ci))
    same_base = same(INV_BASE)
    merge_masks = []
    blk = INV_BASE
    while blk < GDN_CHUNK:
        merge_masks.append(same(2 * blk) & jnp.logical_not(same(blk)))
        blk *= 2
    eye = (ri == ci).astype(f32)
    tri2 = [jnp.concatenate([incl[d].astype(bf16)] * 2, axis=1) for d in range(2)]
    r4 = lax.broadcasted_iota(jnp.int32, (2 * SB, 4 * hd), 0) % SB
    c4 = lax.broadcasted_iota(jnp.int32, (2 * SB, 4 * hd), 1) // hd
    sel = (r4 == (c4 * GDN_HEADS + h)).astype(bf16)
    rh = lax.broadcasted_iota(jnp.int32, (SB, 4 * hd), 0) < GDN_CHUNK
    ch = lax.broadcasted_iota(jnp.int32, (SB, 4 * hd), 1) < 2 * hd
    half_mask = rh == ch

    def phase_a(i, c):
        chains = []
        for u in range(unroll):
            sb = i * unroll + u
            rows = pl.ds(pl.multiple_of(sb * SB, SB), SB)
            kn = kn_ref[rows, :]
            qn = qn_ref[rows, :]
            vn = vn_ref[rows, :]
            knb = kn.astype(bf16)
            kq = lax.dot_general(jnp.concatenate([knb, qn.astype(bf16)], axis=0), knb, NT_DIMS,
                                 preferred_element_type=f32)
            ab_hi, ab_lo = _split_bf16(ab_ref[0, rows, :])
            absel = jnp.dot(jnp.concatenate([ab_hi, ab_lo], axis=1), sel, preferred_element_type=f32)
            for d in range(2):
                chains.append(dict(sb=sb, rows=rows, d=d, kn=kn, qn=qn, vn=vn, kk=kq[:SB], qk=kq[SB:],
                                   a=absel[:, d * hd:(d + 1) * hd], b=absel[:, (2 + d) * hd:(3 + d) * hd]))
        for t in chains:
            d = t["d"]
            xs = t["a"] + dtb_ref[d, h]
            softplus = jnp.maximum(xs, 0.0) + jnp.log1p(jnp.exp(-jnp.abs(xs)))
            g = -jnp.exp(jnp.full((1, hd), alog_ref[d, h], f32)) * softplus
            t["beta"] = _sigmoid(t["b"])
            g_hi, g_lo = _split_bf16(g)
            t["gc"] = jnp.dot(tri2[d], jnp.concatenate([g_hi, g_lo], axis=0),
                              preferred_element_type=f32)
        for t in chains:
            d, gc, beta = t["d"], t["gc"], t["beta"]
            diff = gc - gc.T
            t["decay"] = jnp.where(incl[d], jnp.exp(jnp.where(incl[d], diff, 0.0)), 0.0)
            t["l"] = jnp.where(strict[d], beta * t["kk"] * t["decay"], 0.0)
            t["egc"] = jnp.exp(gc)
            t["rhs"] = jnp.concatenate([beta * t["vn"], (beta * t["egc"]) * t["kn"]], axis=1)
            t["ld"] = jnp.where(same_base, t["l"], 0.0)
            t["ld2"] = _bdot(t["ld"], t["ld"])
        for t in chains:
            iml = eye - t["ld"]
            t["x"] = iml + _bdot(iml, t["ld2"])
            t["ld4"] = _bdot(t["ld2"], t["ld2"])
        for t in chains:
            t["t"] = t["x"] + _bdot(t["x"], t["ld4"])
        for mask in merge_masks:
            for t in chains:
                t["w"] = _bdot(t["t"], jnp.where(mask, t["l"], 0.0))
            for t in chains:
                t["t"] = t["t"] - _bdot(t["w"], t["t"])
        for t in chains:
            t["sol"] = _bdot(t["t"], t["rhs"])
        for t in chains:
            d, gc, sol = t["d"], t["gc"], t["sol"]
            w = jnp.concatenate([sol[:, hd:], sol[:, :hd]], axis=1).astype(bf16)
            qkd = jnp.where(incl[d], t["qk"] * t["decay"], 0.0).astype(bf16)
            t["qw"] = jnp.dot(qkd, w, preferred_element_type=f32)
            last = (GDN_CHUNK - 1) if d == 0 else 0
            gl = jnp.concatenate(
                [jnp.broadcast_to(gc[last:last + 1, :], (GDN_CHUNK, hd)),
                 jnp.broadcast_to(gc[GDN_CHUNK + last:GDN_CHUNK + last + 1, :], (GDN_CHUNK, hd))], axis=0)
            kdt = (t["kn"] * jnp.exp(gl - gc)).T.astype(bf16)
            wpair = jnp.where(half_mask, jnp.concatenate([w, w], axis=1), jnp.zeros((SB, 4 * hd), bf16))
            t["kw"] = jnp.dot(kdt, wpair, preferred_element_type=f32)
            t["egl"] = jnp.exp(gl)
        for t in chains:
            d = t["d"]
            f_, s_ = (0, 1) if d == 0 else (1, 0)
            kw = t["kw"]
            t["qp"] = t["qn"] * t["egc"] - t["qw"][:, :hd]
            a = [-kw[:, 2 * hf * hd:(2 * hf + 1) * hd] for hf in range(2)]
            nn = [kw[:, (2 * hf + 1) * hd:(2 * hf + 2) * hd] for hf in range(2)]
            qs = t["qp"][s_ * GDN_CHUNK:(s_ + 1) * GDN_CHUNK, :]
            t["am"], t["nm"] = a, nn
            t["xc"] = _bdot(jnp.concatenate([a[s_], qs], axis=0), jnp.concatenate([a[f_], nn[f_]], axis=1))
        for t in chains:
            d, sb = t["d"], t["sb"]
            f_, s_ = (0, 1) if d == 0 else (1, 0)
            a, nn, x, qp = t["am"], t["nm"], t["xc"], t["qp"]
            dec = [t["egl"][hf * GDN_CHUNK:hf * GDN_CHUNK + 1, :] for hf in range(2)]
            a2 = dec[s_] * a[f_] + dec[f_] * a[s_] + x[:hd, :hd]
            n2 = dec[s_] * nn[f_] + x[:hd, hd:] + nn[s_]
            q_f = qp[f_ * GDN_CHUNK:(f_ + 1) * GDN_CHUNK, :]
            q_s = dec[f_] * qp[s_ * GDN_CHUNK:(s_ + 1) * GDN_CHUNK, :] + x[hd:, :hd]
            q2 = jnp.concatenate([q_f, q_s] if d == 0 else [q_s, q_f], axis=0)
            zero = jnp.zeros((GDN_CHUNK, hd), f32)
            t["o"] = t["qw"][:, hd:] + jnp.concatenate([zero, x[hd:, hd:]] if d == 0 else [x[hd:, hd:], zero], axis=0)
            l0 = pl.multiple_of(sb * 2 * SB, 2 * SB)
            lhs_ref[d, pl.ds(l0, SB), :] = a2.astype(bf16)
            lhs_ref[d, pl.ds(l0 + SB, SB), :] = q2.astype(bf16)
            n_ref[d, pl.ds(pl.multiple_of(sb * SB, SB), SB), :] = n2
            egl_ref[d, pl.ds(pl.multiple_of(sb * 8, 8), 8), :] = jnp.broadcast_to(dec[0] * dec[1], (8, hd))
        for tf, tb in zip(chains[0::2], chains[1::2]):
            oacc_ref[tf["rows"], :] = tf["o"] + tb["o"]
        return c

    lax.fori_loop(0, nsb // unroll, phase_a, 0)

    def sb_step(d, sbi, state):
        lhs = lhs_ref[d, pl.ds(pl.multiple_of(sbi * 2 * SB, 2 * SB), 2 * SB), :]
        pop = jnp.dot(lhs, state.astype(bf16), preferred_element_type=f32)
        rows = pl.ds(pl.multiple_of(sbi * SB, SB), SB)
        oacc_ref[rows, :] += pop[SB:, :]
        dec = egl_ref[d, pl.ds(pl.multiple_of(sbi * 8, 8), 8), :][0:1, :]
        return dec * state + pop[:SB, :] + n_ref[d, rows, :]

    def phase_b(t, carry):
        sf, sbk = carry
        return sb_step(0, t, sf), sb_step(1, nsb - 1 - t, sbk)

    z0 = jnp.zeros((hd, hd), f32)
    lax.fori_loop(0, nsb, phase_b, (z0, z0))

    nw = nw_ref[...]

    def fin(r, c):
        rows = pl.ds(pl.multiple_of(r * CONV_ROWS, CONV_ROWS), CONV_ROWS)
        o = oacc_ref[rows, :]
        o = o * lax.rsqrt(jnp.mean(o * o, axis=-1, keepdims=True) + RMS_EPS) * nw
        o_ref[0, rows, :] = (o * _silu(z_ref[0, rows, :].astype(f32))).astype(bf16)
        return c

    lax.fori_loop(0, S // CONV_ROWS, fin, 0)


def _gdn(proj3, ab3, conv_w, a_log, dt_bias, norm_w):
    B, S, _ = proj3.shape
    hb = GDN_HEAD_DIM
    nsb = S // SB

    def col(base):
        return pl.BlockSpec((1, S, hb), lambda b, h, base=base: (b, 0, base // hb + h))

    def cw(base):
        return pl.BlockSpec((GDN_CONV, hb), lambda b, h, base=base: (0, base // hb + h))

    smem = pl.BlockSpec(memory_space=pltpu.SMEM)
    return pl.pallas_call(
        functools.partial(_gdn_kernel, S=S, unroll=GDN_UNROLL),
        out_shape=jax.ShapeDtypeStruct((B, S, GDN_WIDTH), bf16),
        grid=(B, GDN_HEADS),
        in_specs=[smem, smem, col(COL_GQ), col(COL_GK), col(COL_GV), col(COL_GZ),
                  pl.BlockSpec((1, S, LANES), lambda b, h: (b, 0, 0)),
                  cw(0), cw(GDN_WIDTH), cw(2 * GDN_WIDTH),
                  pl.BlockSpec((1, hb), lambda b, h: (0, 0))],
        out_specs=pl.BlockSpec((1, S, hb), lambda b, h: (b, 0, h)),
        scratch_shapes=[pltpu.VMEM((S, hb), f32), pltpu.VMEM((S, hb), f32), pltpu.VMEM((S, hb), f32),
                        pltpu.VMEM((3, S + 2 * CONV_PAD, hb), f32),
                        pltpu.VMEM((2, nsb * 2 * SB, hb), bf16),
                        pltpu.VMEM((2, nsb * SB, hb), f32),
                        pltpu.VMEM((2, nsb * 8, hb), f32),
                        pltpu.VMEM((S, hb), f32)],
        compiler_params=_cparams(("parallel", "arbitrary")),
        name="gdn",
    )(a_log, dt_bias, proj3, proj3, proj3, proj3, ab3, conv_w, conv_w, conv_w, norm_w)


N_BIAS_TILES = 6
KV_TILE = 2 * LANES


def _bias_kernel(rb_ref, bucket_ref, lp_ref, bias_ref, lam_ref):
    bucket = bucket_ref[...]
    for hh in range(DIFF_HEADS):
        acc = jnp.zeros(bucket.shape, f32)
        for bk in range(NUM_BUCKETS):
            acc = jnp.where(bucket == bk, rb_ref[bk, hh], acc)
        bias_ref[hh] = acc
    lp = lp_ref[...]
    lam = (jnp.exp(jnp.sum(lp[0:1] * lp[1:2], axis=-1, keepdims=True))
           - jnp.exp(jnp.sum(lp[2:3] * lp[3:4], axis=-1, keepdims=True)) + LAMBDA_INIT)
    lam_ref[...] = jnp.broadcast_to(lam, lam_ref.shape)


def _t5_bucket(rel):
    nb = NUM_BUCKETS // 2
    max_exact = nb // 2
    ret = jnp.where(rel > 0, nb, 0)
    n = jnp.abs(rel)
    nf = jnp.maximum(n, 1).astype(jnp.float32)
    large = max_exact + (jnp.log(nf / max_exact) / math.log(MAX_DISTANCE / max_exact)
                         * (nb - max_exact)).astype(jnp.int32)
    large = jnp.minimum(large, nb - 1)
    return ret + jnp.where(n < max_exact, n, large)


def _bias_tiles(rel_bias, lam_params):
    delta = jnp.arange(N_BIAS_TILES, dtype=jnp.int32)[:, None, None] - 3
    rel = (LANES * delta + jnp.arange(KV_TILE, dtype=jnp.int32)[None, None, :]
           - jnp.arange(LANES, dtype=jnp.int32)[None, :, None])
    bucket = _t5_bucket(rel).astype(jnp.int32)
    return pl.pallas_call(
        _bias_kernel,
        out_shape=(jax.ShapeDtypeStruct((DIFF_HEADS, N_BIAS_TILES, LANES, KV_TILE), f32),
                   jax.ShapeDtypeStruct((8, LANES), f32)),
        in_specs=[pl.BlockSpec(memory_space=pltpu.SMEM),
                  pl.BlockSpec(memory_space=pltpu.VMEM),
                  pl.BlockSpec(memory_space=pltpu.VMEM)],
        out_specs=(pl.BlockSpec(memory_space=pltpu.VMEM), pl.BlockSpec(memory_space=pltpu.VMEM)),
        name="rel_bias",
    )(rel_bias, bucket, lam_params)


def _diff_kernel(q_ref, k_ref, v_ref, bias_ref, lam_ref, nw_ref, o_ref, s_ref, e_ref, *, S, QB):
    qi = pl.program_id(2)
    n_tiles = S // KV_TILE
    nsub = QB // LANES
    dh = DIFF_HEAD_DIM
    q = (q_ref[0].astype(f32) * (dh ** -0.5)).astype(bf16)

    def tile(jt):
        return slice(jt * KV_TILE, (jt + 1) * KV_TILE)

    def score_tile(m, jt, mx):
        s = lax.dot_general(q[:, m * dh:(m + 1) * dh], k_ref[0, tile(jt), m * dh:(m + 1) * dh], NT_DIMS,
                            preferred_element_type=f32)
        parts = []
        for r in range(nsub):
            idx = jnp.clip(2 * jt - (qi * nsub + r), -3, 2) + 3
            parts.append(s[r * LANES:(r + 1) * LANES, :] + bias_ref[0, idx])
        s = jnp.concatenate(parts, axis=0) if nsub > 1 else parts[0]
        s_ref[m, :, tile(jt)] = s
        return jnp.maximum(mx, jnp.maximum(s[:, :LANES], s[:, LANES:]))

    def exp_tile(m, jt, row_max, den):
        e = jnp.exp(s_ref[m, :, tile(jt)] - row_max)
        e_ref[m, :, tile(jt)] = e.astype(bf16)
        return den + (e[:, :LANES] + e[:, LANES:])

    def pv_tile(m, jt, acc):
        return acc + jnp.dot(e_ref[m, :, tile(jt)], v_ref[0, tile(jt), :], preferred_element_type=f32)

    neg = jnp.full((QB, LANES), -jnp.inf, f32)
    zero = jnp.zeros((QB, LANES), f32)
    mx0 = neg
    for jt in range(n_tiles):
        mx0 = score_tile(0, jt, mx0)
    max0 = jnp.max(mx0, axis=-1, keepdims=True)
    mx1, den0 = neg, zero
    for jt in range(n_tiles):
        mx1 = score_tile(1, jt, mx1)
        den0 = exp_tile(0, jt, max0, den0)
    max1 = jnp.max(mx1, axis=-1, keepdims=True)
    pv0, den1 = jnp.zeros((QB, 2 * dh), f32), zero
    for jt in range(n_tiles):
        pv0 = pv_tile(0, jt, pv0)
        den1 = exp_tile(1, jt, max1, den1)
    pv1 = jnp.zeros((QB, 2 * dh), f32)
    for jt in range(n_tiles):
        pv1 = pv_tile(1, jt, pv1)
    outs = [pv0 / jnp.sum(den0, axis=-1, keepdims=True), pv1 / jnp.sum(den1, axis=-1, keepdims=True)]
    lam = lam_ref[0:1, 0:1]
    o = outs[0] - lam * outs[1]
    o = o * lax.rsqrt(jnp.mean(o * o, axis=-1, keepdims=True) + RMS_EPS) * nw_ref[...]
    o_ref[0] = (o * (1.0 - LAMBDA_INIT)).astype(bf16)


def _diff_attention(proj3, bias_tiles, lam, norm_w, *, qb):
    B, S, _ = proj3.shape
    w = 2 * DIFF_HEAD_DIM
    return pl.pallas_call(
        functools.partial(_diff_kernel, S=S, QB=qb),
        out_shape=jax.ShapeDtypeStruct((B, S, DIFF_HEADS * w), bf16),
        grid=(B, DIFF_HEADS, S // qb),
        in_specs=[pl.BlockSpec((1, qb, w), lambda b, h, i: (b, i, COL_DQ // w + h)),
                  pl.BlockSpec((1, S, w), lambda b, h, i: (b, 0, COL_DK // w + h)),
                  pl.BlockSpec((1, S, w), lambda b, h, i: (b, 0, COL_DV // w + h)),
                  pl.BlockSpec((1, N_BIAS_TILES, LANES, KV_TILE), lambda b, h, i: (h, 0, 0, 0)),
                  pl.BlockSpec((8, LANES), lambda b, h, i: (0, 0)),
                  pl.BlockSpec((1, w), lambda b, h, i: (0, 0))],
        out_specs=pl.BlockSpec((1, qb, w), lambda b, h, i: (b, i, h)),
        scratch_shapes=[pltpu.VMEM((2, qb, S), f32), pltpu.VMEM((2, qb, S), bf16)],
        compiler_params=_cparams(("parallel", "parallel", "arbitrary")),
        name="diff_attn",
    )(proj3, proj3, proj3, bias_tiles, lam, norm_w)


def _cross_kernel(q_ref, kv_ref, o_ref):
    w = CROSS_HEAD_DIM
    heads = range(CROSS_HEADS)
    s = [lax.dot_general(q_ref[0, :, h * w:(h + 1) * w], kv_ref[0, :, h * w:(h + 1) * w], NT_DIMS,
                         preferred_element_type=f32) * (w ** -0.5) for h in heads]
    e = [jnp.exp(s[h] - jnp.max(s[h], axis=-1, keepdims=True)) for h in heads]
    p = [(e[h] / jnp.sum(e[h], axis=-1, keepdims=True)).astype(bf16) for h in heads]
    for h in heads:
        o_ref[0, :, h * w:(h + 1) * w] = jnp.dot(
            p[h], kv_ref[0, :, (CROSS_HEADS + h) * w:(CROSS_HEADS + h + 1) * w],
            preferred_element_type=f32).astype(bf16)


def _cross_attention(proj3, kv3, *, qb):
    B, S, _ = proj3.shape
    M = kv3.shape[1]
    cw = CROSS_HEADS * CROSS_HEAD_DIM
    return pl.pallas_call(
        _cross_kernel,
        out_shape=jax.ShapeDtypeStruct((B, S, cw), bf16),
        grid=(B, S // qb),
        in_specs=[pl.BlockSpec((1, qb, cw), lambda b, i: (b, i, COL_CQ // cw)),
                  pl.BlockSpec((1, M, 2 * cw), lambda b, i: (b, 0, 0))],
        out_specs=pl.BlockSpec((1, qb, cw), lambda b, i: (b, i, 0)),
        compiler_params=_cparams(("parallel", "arbitrary")),
        name="cross_attn",
    )(proj3, kv3)


def _merge_kernel(yg_ref, yd_ref, yc_ref, g0_ref, g1_ref, g2_ref, wg_ref, wd_ref, wc_ref, o_ref):
    m = g0_ref[...].astype(f32) * jnp.dot(yg_ref[...], wg_ref[...], preferred_element_type=f32)
    m = m + g1_ref[...].astype(f32) * jnp.dot(yd_ref[...], wd_ref[...], preferred_element_type=f32)
    m = m + g2_ref[...].astype(f32) * jnp.dot(yc_ref[...], wc_ref[...], preferred_element_type=f32)
    o_ref[...] = m.astype(bf16)


def _merge(yg, yd, yc, proj2, wg, wd, wc, *, tm):
    T = yg.shape[0]
    gate_blk = COL_GATE // D_MODEL

    def yspec():
        return pl.BlockSpec((tm, GDN_WIDTH), lambda i: (i, 0))

    def gspec(n):
        return pl.BlockSpec((tm, D_MODEL), lambda i, n=n: (i, gate_blk + n))

    def wspec():
        return pl.BlockSpec((GDN_WIDTH, D_MODEL), lambda i: (0, 0), pipeline_mode=pl.Buffered(1))

    return pl.pallas_call(
        _merge_kernel,
        out_shape=jax.ShapeDtypeStruct((T, D_MODEL), bf16),
        grid=(T // tm,),
        in_specs=[yspec(), yspec(), yspec(), gspec(0), gspec(1), gspec(2), wspec(), wspec(), wspec()],
        out_specs=pl.BlockSpec((tm, D_MODEL), lambda i: (i, 0)),
        compiler_params=_cparams(("parallel",)),
        name="merge",
    )(yg, yd, yc, proj2, proj2, proj2, wg, wd, wc)


def _outproj_kernel(m_ref, x_ref, w_ref, g_ref, b_ref, o_ref):
    y = DEEPNORM_ALPHA * x_ref[...] + jnp.dot(m_ref[...], w_ref[...], preferred_element_type=f32)
    o_ref[...] = _layer_norm(y, g_ref[...], b_ref[...])


def _outproj_ln(merged, x2, w_out, g, b, *, tm):
    T = x2.shape[0]
    return pl.pallas_call(
        _outproj_kernel,
        out_shape=jax.ShapeDtypeStruct((T, D_MODEL), f32),
        grid=(T // tm,),
        in_specs=[pl.BlockSpec((tm, D_MODEL), lambda i: (i, 0)),
                  pl.BlockSpec((tm, D_MODEL), lambda i: (i, 0)),
                  pl.BlockSpec((D_MODEL, D_MODEL), lambda i: (0, 0), pipeline_mode=pl.Buffered(1)),
                  pl.BlockSpec((1, D_MODEL), lambda i: (0, 0)),
                  pl.BlockSpec((1, D_MODEL), lambda i: (0, 0))],
        out_specs=pl.BlockSpec((tm, D_MODEL), lambda i: (i, 0)),
        compiler_params=_cparams(("parallel",)),
        name="outproj_ln",
    )(merged, x2, w_out, g, b)


HALO = 16
FFN_TN = 512
FFN_SUB = 128
FFN_ROWS = 128


def _ffn_kernel(xm_ref, xp_ref, xn_ref, wg_ref, wv_ref, cg_ref, cv_ref, wdn_ref, g_ref, b_ref, o_ref,
                xh_ref, up_ref, h_ref, *, tm, tn, tiles_per_seq):
    i = pl.program_id(0)
    n = pl.program_id(1)

    @pl.when(n == 0)
    def _():
        pos = i % tiles_per_seq
        keep_prev = (pos != 0).astype(f32)
        keep_next = (pos != tiles_per_seq - 1).astype(f32)
        xh_ref[0:HALO, :] = (xp_ref[...] * keep_prev).astype(bf16)
        xh_ref[HALO:HALO + tm, :] = xm_ref[...].astype(bf16)
        xh_ref[HALO + tm:2 * HALO + tm, :] = (xn_ref[...] * keep_next).astype(bf16)
        o_ref[...] = DEEPNORM_ALPHA * xm_ref[...]

    taps = (cg_ref[...], cv_ref[...])
    planes = FFN_SUB // LANES

    def up_dot(c):
        cols = slice(c * FFN_SUB, (c + 1) * FFN_SUB)
        w = jnp.concatenate([wg_ref[:, cols], wv_ref[:, cols]], axis=1)
        up = jnp.dot(xh_ref[...], w, preferred_element_type=f32)
        for q in range(2 * planes):
            up_ref[c % 2, q] = up[:, q * LANES:(q + 1) * LANES]

    def act(c):
        for p in range(planes):
            c0 = c * FFN_SUB + p * LANES
            for r in range(tm // FFN_ROWS):
                r0 = r * FFN_ROWS
                conv = [sum(up_ref[c % 2, gv * planes + p, r0 + HALO - 1 + j:r0 + HALO - 1 + j + FFN_ROWS, :]
                            * taps[gv][j:j + 1, c0:c0 + LANES] for j in range(3)) for gv in range(2)]
                h_ref[r0:r0 + FFN_ROWS, c0:c0 + LANES] = (_silu(conv[0]) * conv[1]).astype(bf16)

    nsub = tn // FFN_SUB
    up_dot(0)
    for c in range(nsub):
        if c + 1 < nsub:
            up_dot(c + 1)
        act(c)
    o_ref[...] += jnp.dot(h_ref[...], wdn_ref[...], preferred_element_type=f32)

    @pl.when(n == pl.num_programs(1) - 1)
    def _():
        o_ref[...] = _layer_norm(o_ref[...], g_ref[...], b_ref[...])


def _ffn_ln(x1, w_up_p, conv_p, w_down_p, g, b, *, S, tm, tn):
    T = x1.shape[0]
    nf = D_FF_PAD // tn
    hpt = tm // HALO
    last_halo = T // HALO - 1
    return pl.pallas_call(
        functools.partial(_ffn_kernel, tm=tm, tn=tn, tiles_per_seq=S // tm),
        out_shape=jax.ShapeDtypeStruct((T, D_MODEL), f32),
        grid=(T // tm, nf),
        in_specs=[pl.BlockSpec((tm, D_MODEL), lambda i, n: (i, 0)),
                  pl.BlockSpec((HALO, D_MODEL), lambda i, n: (jnp.maximum(i * hpt - 1, 0), 0)),
                  pl.BlockSpec((HALO, D_MODEL), lambda i, n: (jnp.minimum((i + 1) * hpt, last_halo), 0)),
                  pl.BlockSpec((D_MODEL, tn), lambda i, n: (0, n)),
                  pl.BlockSpec((D_MODEL, tn), lambda i, n: (0, nf + n)),
                  pl.BlockSpec((3, tn), lambda i, n: (0, n)),
                  pl.BlockSpec((3, tn), lambda i, n: (0, nf + n)),
                  pl.BlockSpec((tn, D_MODEL), lambda i, n: (n, 0)),
                  pl.BlockSpec((1, D_MODEL), lambda i, n: (0, 0)),
                  pl.BlockSpec((1, D_MODEL), lambda i, n: (0, 0))],
        out_specs=pl.BlockSpec((tm, D_MODEL), lambda i, n: (i, 0)),
        scratch_shapes=[pltpu.VMEM((tm + 2 * HALO, D_MODEL), bf16),
                        pltpu.VMEM((2, 2 * FFN_SUB // LANES, tm + 2 * HALO, LANES), f32),
                        pltpu.VMEM((tm, tn), bf16)],
        compiler_params=_cparams(("parallel", "arbitrary")),
        name="ffn_ln",
    )(x1, x1, x1, w_up_p, w_up_p, conv_p, conv_p, w_down_p, g, b)


def _pick(n, prefs):
    for p in prefs:
        if n % p == 0:
            return p
    raise ValueError(f"no tile in {prefs} divides {n}")


def _trunk(x, mem, wts):
    B, S, _ = x.shape
    T = B * S
    assert S % CONV_ROWS == 0 and S % KV_TILE == 0 and (S // SB) % GDN_UNROLL == 0
    x2 = x.reshape(T, D_MODEL)

    proj2, ab2 = _inproj(x2, wts["w_all"], wts["b_all"], wts["w_ab"], tm=_pick(T, (1024, 512, 256)), tn=1024)
    proj3 = proj2.reshape(B, S, PROJ_COLS)
    ab3 = ab2.reshape(B, S, LANES)

    y_gdn = _gdn(proj3, ab3, wts["gdn_conv"], wts["a_log"], wts["dt_bias"], wts["gdn_norm_w"])
    y_diff = _diff_attention(proj3, wts["bias_tiles"], wts["lam"], wts["diff_norm_w"], qb=_pick(S, (512, 256)))
    M = mem.shape[1]
    kv = _matmul(mem.reshape(B * M, D_MODEL), wts["w_mem_kv"], tm=_pick(B * M, (512, 256)))
    y_cross = _cross_attention(proj3, kv.reshape(B, M, 2 * CROSS_HEADS * CROSS_HEAD_DIM), qb=_pick(S, (1024, 512, 256)))

    tm = _pick(S, (512, 256))
    merged = _merge(y_gdn.reshape(T, -1), y_diff.reshape(T, -1), y_cross.reshape(T, -1), proj2,
                    wts["w_bg"], wts["w_bd"], wts["w_bc"], tm=tm)
    x1 = _outproj_ln(merged, x2, wts["w_out"], wts["ln1_g"], wts["ln1_b"], tm=tm)
    y = _ffn_ln(x1, wts["w_up_p"], wts["ffn_conv_p"], wts["w_down_p"], wts["ln2_g"], wts["ln2_b"],
                S=S, tm=tm, tn=FFN_TN)
    return y.reshape(B, S, D_MODEL)


def _prep_weights(rel_bias, w_in, gdn_conv, gdn_a_log, gdn_dt_bias, gdn_norm_w, diff_lambda, diff_norm_w,
                  w_mem_kv, w_gate, b_gate, w_branch_gdn, w_branch_diff, w_branch_cross, w_out,
                  ln1_g, ln1_b, w_up, ffn_conv, w_down, ln2_g, ln2_b):
    l = 0
    wi = w_in[l]
    a0 = 4 * GDN_WIDTH
    a1 = a0 + 4 * GDN_HEADS
    w_lin = jnp.concatenate([wi[:, :a0], wi[:, a1:]], axis=1)
    w_all = jnp.concatenate([w_lin, w_gate[l]], axis=1).astype(bf16)
    b_all = jnp.concatenate([jnp.zeros((PROJ_LIN,), f32), b_gate[l].astype(f32)])[None, :]
    w_ab = jnp.pad(wi[:, a0:a1], ((0, 0), (0, LANES - 4 * GDN_HEADS))).astype(bf16)

    padc = D_FF_PAD - D_FF

    def pad_halves(a):
        return jnp.pad(a.reshape(a.shape[0], 2, D_FF), ((0, 0), (0, 0), (0, padc))).reshape(a.shape[0], 2 * D_FF_PAD)

    bias_tiles, lam = _bias_tiles(rel_bias.astype(f32), diff_lambda[l].astype(f32))
    return dict(
        w_all=w_all, b_all=b_all, w_ab=w_ab,
        gdn_conv=gdn_conv[l].astype(f32), a_log=gdn_a_log[l].astype(f32), dt_bias=gdn_dt_bias[l].astype(f32),
        gdn_norm_w=gdn_norm_w[l].astype(f32)[None, :],
        bias_tiles=bias_tiles, lam=lam, diff_norm_w=diff_norm_w[l].astype(f32)[None, :],
        w_mem_kv=w_mem_kv[l].astype(bf16),
        w_bg=w_branch_gdn[l].astype(bf16), w_bd=w_branch_diff[l].astype(bf16), w_bc=w_branch_cross[l].astype(bf16),
        w_out=w_out[l].astype(bf16),
        ln1_g=ln1_g[l].astype(f32)[None, :], ln1_b=ln1_b[l].astype(f32)[None, :],
        w_up_p=pad_halves(w_up[l].astype(bf16)), ffn_conv_p=pad_halves(ffn_conv[l].astype(f32)),
        w_down_p=jnp.pad(w_down[l], ((0, padc), (0, 0))).astype(bf16),
        ln2_g=ln2_g[l].astype(f32)[None, :], ln2_b=ln2_b[l].astype(f32)[None, :],
    )


def kernel(x_prompt, x_sample, mem_prompt, mem_sample, rel_bias, w_in, gdn_conv, gdn_a_log, gdn_dt_bias, gdn_norm_w, diff_lambda, diff_norm_w, w_mem_kv, w_gate, b_gate, w_branch_gdn, w_branch_diff, w_branch_cross, w_out, ln1_g, ln1_b, w_up, ffn_conv, w_down, ln2_g, ln2_b):
    wts = _prep_weights(rel_bias, w_in, gdn_conv, gdn_a_log, gdn_dt_bias, gdn_norm_w, diff_lambda, diff_norm_w,
                        w_mem_kv, w_gate, b_gate, w_branch_gdn, w_branch_diff, w_branch_cross, w_out,
                        ln1_g, ln1_b, w_up, ffn_conv, w_down, ln2_g, ln2_b)
    return (_trunk(x_prompt, mem_prompt, wts), _trunk(x_sample, mem_sample, wts))
```

```python
import functools
import math

import jax
import jax.numpy as jnp
import numpy as np
from jax import lax
from jax.experimental import pallas as pl
from jax.experimental.pallas import tpu as pltpu

f32 = jnp.float32
bf16 = jnp.bfloat16

D_MODEL = 2048
GDN_HEADS = 8
GDN_HEAD_DIM = 128
GDN_WIDTH = 1024
GDN_CONV = 5
GDN_CHUNK = 64
DIFF_HEADS = 4
DIFF_HEAD_DIM = 128
CROSS_HEADS = 4
CROSS_HEAD_DIM = 256
N_BRANCH = 3
D_FF = 5504
NUM_BUCKETS = 32
MAX_DISTANCE = 128
LN_EPS = 1e-5
RMS_EPS = 1e-6
L2_EPS = 1e-6
DEPTH = 1
DEEPNORM_ALPHA = (2 * DEPTH) ** 0.25
LAMBDA_INIT = 0.8 - 0.6 * math.exp(-0.3 * 0)

LANES = 128
MXU_WIDTH = 256
VMEM_LIMIT_BYTES = 56 * 1024 * 1024

PROJ_LIN = 8192
COL_GQ, COL_GK, COL_GV, COL_GZ = 0, 1024, 2048, 3072
COL_DQ, COL_DK, COL_DV, COL_CQ = 4096, 5120, 6144, 7168
COL_GATE = PROJ_LIN
PROJ_COLS = PROJ_LIN + N_BRANCH * D_MODEL
D_FF_PAD = 5632

NT_DIMS = (((1,), (1,)), ((), ()))


def _cparams(sem, vmem=VMEM_LIMIT_BYTES):
    return pltpu.CompilerParams(dimension_semantics=sem, vmem_limit_bytes=vmem)


def _sigmoid(x):
    return 1.0 / (1.0 + jnp.exp(-x))


def _silu(x):
    return x * _sigmoid(x)


def _layer_norm(y, g, b):
    mu = jnp.mean(y, axis=-1, keepdims=True)
    yc = y - mu
    var = jnp.mean(yc * yc, axis=-1, keepdims=True)
    return yc * lax.rsqrt(var + LN_EPS) * g + b


def _bdot(a, b):
    return jnp.dot(a.astype(bf16), b.astype(bf16), preferred_element_type=f32)


def _inproj_kernel(x_ref, w_ref, b_ref, wab_ref, o_ref, ab_ref, xb_ref, *, n_lin_tiles):
    j = pl.program_id(1)

    @pl.when(j == 0)
    def _():
        xb = x_ref[...].astype(bf16)
        xb_ref[...] = xb
        ab_ref[...] = jnp.dot(xb, wab_ref[...], preferred_element_type=f32)

    acc = jnp.dot(xb_ref[...], w_ref[...], preferred_element_type=f32)
    o_ref[...] = jnp.where(j >= n_lin_tiles, _sigmoid(acc + b_ref[...]), acc).astype(bf16)


def _inproj(x2, w_all, b_all, w_ab, *, tm, tn):
    T = x2.shape[0]
    n_lin_tiles = PROJ_LIN // tn
    return pl.pallas_call(
        functools.partial(_inproj_kernel, n_lin_tiles=n_lin_tiles),
        out_shape=(jax.ShapeDtypeStruct((T, PROJ_COLS), bf16),
                   jax.ShapeDtypeStruct((T, LANES), f32)),
        grid=(T // tm, PROJ_COLS // tn),
        in_specs=[pl.BlockSpec((tm, D_MODEL), lambda i, j: (i, 0)),
                  pl.BlockSpec((D_MODEL, tn), lambda i, j: (0, j)),
                  pl.BlockSpec((1, tn), lambda i, j: (0, j)),
                  pl.BlockSpec((D_MODEL, LANES), lambda i, j: (0, 0))],
        out_specs=(pl.BlockSpec((tm, tn), lambda i, j: (i, j)),
                   pl.BlockSpec((tm, LANES), lambda i, j: (i, 0))),
        scratch_shapes=[pltpu.VMEM((tm, D_MODEL), bf16)],
        compiler_params=_cparams(("parallel", "arbitrary")),
        name="inproj",
    )(x2, w_all, b_all, w_ab)


def _mm_kernel(x_ref, w_ref, o_ref):
    o_ref[...] = jnp.dot(x_ref[...].astype(bf16), w_ref[...], preferred_element_type=f32).astype(o_ref.dtype)


def _matmul(x2, w, *, tm, out_dtype=bf16):
    T, K = x2.shape
    N = w.shape[1]
    return pl.pallas_call(
        _mm_kernel,
        out_shape=jax.ShapeDtypeStruct((T, N), out_dtype),
        grid=(T // tm,),
        in_specs=[pl.BlockSpec((tm, K), lambda i: (i, 0)),
                  pl.BlockSpec((K, N), lambda i: (0, 0), pipeline_mode=pl.Buffered(1))],
        out_specs=pl.BlockSpec((tm, N), lambda i: (i, 0)),
        compiler_params=_cparams(("parallel",)),
        name="mem_kv",
    )(x2, w)


SB = 2 * GDN_CHUNK
CONV_ROWS = 256
CONV_PAD = 8
GDN_UNROLL = 8
INV_BASE = 8


def _split_bf16(x):
    hi = x.astype(bf16)
    lo = (x - hi.astype(f32)).astype(bf16)
    return hi, lo


def _gdn_kernel(alog_ref, dtb_ref, q_ref, k_ref, v_ref, z_ref, ab_ref, cq_ref, ck_ref, cv_ref,
                nw_ref, o_ref, qn_ref, kn_ref, vn_ref, xp_ref, lhs_ref, n_ref, egl_ref, oacc_ref, *, S, unroll):
    h = pl.program_id(1)
    nsb = S // SB
    hd = GDN_HEAD_DIM

    zpad = jnp.zeros((CONV_PAD, hd), f32)
    streams = ((q_ref, cq_ref, qn_ref, "q"), (k_ref, ck_ref, kn_ref, "k"), (v_ref, cv_ref, vn_ref, "v"))
    for si, (src_ref, _, _, _) in enumerate(streams):
        xp_ref[si, 0:CONV_PAD, :] = zpad
        xp_ref[si, S + CONV_PAD:S + 2 * CONV_PAD, :] = zpad
        xp_ref[si, CONV_PAD:S + CONV_PAD, :] = src_ref[0].astype(f32)
    taps = [w_ref[...] for _, w_ref, _, _ in streams]

    def conv_body(r, c):
        r0 = pl.multiple_of(r * CONV_ROWS, CONV_ROWS)
        for si, (_, _, dst_ref, mode) in enumerate(streams):
            w = taps[si]
            acc = jnp.zeros((CONV_ROWS, hd), f32)
            for j in range(GDN_CONV):
                acc = acc + xp_ref[si, pl.ds(r0 + CONV_PAD - GDN_CONV // 2 + j, CONV_ROWS), :] * w[j:j + 1, :]
            y = _silu(acc)
            if mode != "v":
                y = y * lax.rsqrt(jnp.sum(y * y, axis=-1, keepdims=True) + L2_EPS)
            if mode == "q":
                y = y * (hd ** -0.5)
            dst_ref[pl.ds(r0, CONV_ROWS), :] = y
        return c

    lax.fori_loop(0, S // CONV_ROWS, conv_body, 0)

    ri = lax.broadcasted_iota(jnp.int32, (SB, SB), 0)
    ci = lax.broadcasted_iota(jnp.int32, (SB, SB), 1)

    def same(blk):
        return (ri // blk) == (ci // blk)

    same_chunk = same(GDN_CHUNK)
    incl = (same_chunk & (ri >= ci), same_chunk & (ri <= ci))
    strict = (same_chunk & (ri > ci), same_chunk & (ri < ci))
    same_base = same(INV_BASE)
    merge_masks = []
    blk = INV_BASE
    while blk < GDN_CHUNK:
        merge_masks.append(same(2 * blk) & jnp.logical_not(same(blk)))
        blk *= 2
    eye = (ri == ci).astype(f32)
    tri2 = [jnp.concatenate([incl[d].astype(bf16)] * 2, axis=1) for d in range(2)]
    r4 = lax.broadcasted_iota(jnp.int32, (2 * SB, 4 * hd), 0) % SB
    c4 = lax.broadcasted_iota(jnp.int32, (2 * SB, 4 * hd), 1) // hd
    sel = (r4 == (c4 * GDN_HEADS + h)).astype(bf16)
    rh = lax.broadcasted_iota(jnp.int32, (SB, 4 * hd), 0) < GDN_CHUNK
    ch = lax.broadcasted_iota(jnp.int32, (SB, 4 * hd), 1) < 2 * hd
    half_mask = rh == ch

    ends = unroll // 2
    n_groups = nsb // unroll

    def chunk_local(i, rec=()):
        rec = list(rec)

        def run_rec():
            if rec:
                rec.pop(0)()

        chains = []
        for u in range(unroll):
            sb = i * ends + u if u < ends else nsb - (i + 1) * ends + (u - ends)
            rows = pl.ds(pl.multiple_of(sb * SB, SB), SB)
            kn = kn_ref[rows, :]
            qn = qn_ref[rows, :]
            vn = vn_ref[rows, :]
            knb = kn.astype(bf16)
            kq = lax.dot_general(jnp.concatenate([knb, qn.astype(bf16)], axis=0), knb, NT_DIMS,
                                 preferred_element_type=f32)
            ab_hi, ab_lo = _split_bf16(ab_ref[0, rows, :])
            absel = jnp.dot(jnp.concatenate([ab_hi, ab_lo], axis=1), sel, preferred_element_type=f32)
            for d in range(2):
                chains.append(dict(sb=sb, rows=rows, d=d, kn=kn, qn=qn, vn=vn, kk=kq[:SB], qk=kq[SB:],
                                   a=absel[:, d * hd:(d + 1) * hd], b=absel[:, (2 + d) * hd:(3 + d) * hd]))
        for t in chains:
            d = t["d"]
            xs = t["a"] + dtb_ref[d, h]
            softplus = jnp.maximum(xs, 0.0) + jnp.log1p(jnp.exp(-jnp.abs(xs)))
            g = -jnp.exp(jnp.full((1, hd), alog_ref[d, h], f32)) * softplus
            t["beta"] = _sigmoid(t["b"])
            g_hi, g_lo = _split_bf16(g)
            t["gc"] = jnp.dot(tri2[d], jnp.concatenate([g_hi, g_lo], axis=0),
                              preferred_element_type=f32)
        run_rec()
        for t in chains:
            d, gc, beta = t["d"], t["gc"], t["beta"]
            diff = gc - gc.T
            t["decay"] = jnp.where(incl[d], jnp.exp(jnp.where(incl[d], diff, 0.0)), 0.0)
            t["l"] = jnp.where(strict[d], beta * t["kk"] * t["decay"], 0.0)
            t["egc"] = jnp.exp(gc)
            t["rhs"] = jnp.concatenate([beta * t["vn"], (beta * t["egc"]) * t["kn"]], axis=1)
            t["ld"] = jnp.where(same_base, t["l"], 0.0)
            t["ld2"] = _bdot(t["ld"], t["ld"])
        run_rec()
        for t in chains:
            iml = eye - t["ld"]
            t["x"] = iml + _bdot(iml, t["ld2"])
            t["ld4"] = _bdot(t["ld2"], t["ld2"])
        run_rec()
        for t in chains:
            t["t"] = t["x"] + _bdot(t["x"], t["ld4"])
        run_rec()
        for mask in merge_masks:
            for t in chains:
                t["w"] = _bdot(t["t"], jnp.where(mask, t["l"], 0.0))
            run_rec()
            for t in chains:
                t["t"] = t["t"] - _bdot(t["w"], t["t"])
            run_rec()
        for t in chains:
            t["sol"] = _bdot(t["t"], t["rhs"])
        while rec:
            run_rec()
        for t in chains:
            d, gc, sol = t["d"], t["gc"], t["sol"]
            w = jnp.concatenate([sol[:, hd:], sol[:, :hd]], axis=1).astype(bf16)
            qkd = jnp.where(incl[d], t["qk"] * t["decay"], 0.0).astype(bf16)
            t["qw"] = jnp.dot(qkd, w, preferred_element_type=f32)
            last = (GDN_CHUNK - 1) if d == 0 else 0
            gl = jnp.concatenate(
                [jnp.broadcast_to(gc[last:last + 1, :], (GDN_CHUNK, hd)),
                 jnp.broadcast_to(gc[GDN_CHUNK + last:GDN_CHUNK + last + 1, :], (GDN_CHUNK, hd))], axis=0)
            kdt = (t["kn"] * jnp.exp(gl - gc)).T.astype(bf16)
            wpair = jnp.where(half_mask, jnp.concatenate([w, w], axis=1), jnp.zeros((SB, 4 * hd), bf16))
            t["kw"] = jnp.dot(kdt, wpair, preferred_element_type=f32)
            t["egl"] = jnp.exp(gl)
        for t in chains:
            d = t["d"]
            f_, s_ = (0, 1) if d == 0 else (1, 0)
            kw = t["kw"]
            t["qp"] = t["qn"] * t["egc"] - t["qw"][:, :hd]
            a = [-kw[:, 2 * hf * hd:(2 * hf + 1) * hd] for hf in range(2)]
            nn = [kw[:, (2 * hf + 1) * hd:(2 * hf + 2) * hd] for hf in range(2)]
            qs = t["qp"][s_ * GDN_CHUNK:(s_ + 1) * GDN_CHUNK, :]
            t["am"], t["nm"] = a, nn
            t["xc"] = _bdot(jnp.concatenate([a[s_], qs], axis=0), jnp.concatenate([a[f_], nn[f_]], axis=1))
        for t in chains:
            d, sb = t["d"], t["sb"]
            f_, s_ = (0, 1) if d == 0 else (1, 0)
            a, nn, x, qp = t["am"], t["nm"], t["xc"], t["qp"]
            dec = [t["egl"][hf * GDN_CHUNK:hf * GDN_CHUNK + 1, :] for hf in range(2)]
            a2 = dec[s_] * a[f_] + dec[f_] * a[s_] + x[:hd, :hd]
            n2 = dec[s_] * nn[f_] + x[:hd, hd:] + nn[s_]
            q_f = qp[f_ * GDN_CHUNK:(f_ + 1) * GDN_CHUNK, :]
            q_s = dec[f_] * qp[s_ * GDN_CHUNK:(s_ + 1) * GDN_CHUNK, :] + x[hd:, :hd]
            q2 = jnp.concatenate([q_f, q_s] if d == 0 else [q_s, q_f], axis=0)
            zero = jnp.zeros((GDN_CHUNK, hd), f32)
            t["o"] = t["qw"][:, hd:] + jnp.concatenate([zero, x[hd:, hd:]] if d == 0 else [x[hd:, hd:], zero], axis=0)
            l0 = pl.multiple_of(sb * 2 * SB, 2 * SB)
            lhs_ref[d, pl.ds(l0, SB), :] = a2.astype(bf16)
            lhs_ref[d, pl.ds(l0 + SB, SB), :] = q2.astype(bf16)
            n_ref[d, pl.ds(pl.multiple_of(sb * SB, SB), SB), :] = n2
            egl_ref[d, pl.ds(pl.multiple_of(sb * 8, 8), 8), :] = jnp.broadcast_to(dec[0] * dec[1], (8, hd))
        for tf, tb in zip(chains[0::2], chains[1::2]):
            oacc_ref[tf["rows"], :] = tf["o"] + tb["o"]

    def sb_step(d, sbi, state):
        lhs = lhs_ref[d, pl.ds(pl.multiple_of(sbi * 2 * SB, 2 * SB), 2 * SB), :]
        pop = jnp.dot(lhs, state.astype(bf16), preferred_element_type=f32)
        rows = pl.ds(pl.multiple_of(sbi * SB, SB), SB)
        oacc_ref[rows, :] += pop[SB:, :]
        dec = egl_ref[d, pl.ds(pl.multiple_of(sbi * 8, 8), 8), :][0:1, :]
        return dec * state + pop[:SB, :] + n_ref[d, rows, :]

    def rec_steps(g, states):
        def step(j):
            def run():
                sf, sbk = states[0]
                t = g * ends + j
                states[0] = (sb_step(0, t, sf), sb_step(1, nsb - 1 - t, sbk))
            return run
        return [step(j) for j in range(ends)]

    z0 = jnp.zeros((hd, hd), f32)
    chunk_local(0)

    def piped(i, carry):
        states = [carry]
        chunk_local(i, rec_steps(i - 1, states))
        return states[0]

    carry = lax.fori_loop(1, n_groups, piped, (z0, z0))
    states = [carry]
    for run in rec_steps(n_groups - 1, states):
        run()

    def tail(t, carry):
        sf, sbk = carry
        return sb_step(0, t, sf), sb_step(1, nsb - 1 - t, sbk)

    lax.fori_loop(nsb // 2, nsb, tail, states[0])

    nw = nw_ref[...]

    def fin(r, c):
        rows = pl.ds(pl.multiple_of(r * CONV_ROWS, CONV_ROWS), CONV_ROWS)
        o = oacc_ref[rows, :]
        o = o * lax.rsqrt(jnp.mean(o * o, axis=-1, keepdims=True) + RMS_EPS) * nw
        o_ref[0, rows, :] = (o * _silu(z_ref[0, rows, :].astype(f32))).astype(bf16)
        return c

    lax.fori_loop(0, S // CONV_ROWS, fin, 0)


def _gdn(proj3, ab3, conv_w, a_log, dt_bias, norm_w):
    B, S, _ = proj3.shape
    hb = GDN_HEAD_DIM
    nsb = S // SB

    def col(base):
        return pl.BlockSpec((1, S, hb), lambda b, h, base=base: (b, 0, base // hb + h))

    def cw(base):
        return pl.BlockSpec((GDN_CONV, hb), lambda b, h, base=base: (0, base // hb + h))

    smem = pl.BlockSpec(memory_space=pltpu.SMEM)
    return pl.pallas_call(
        functools.partial(_gdn_kernel, S=S, unroll=GDN_UNROLL),
        out_shape=jax.ShapeDtypeStruct((B, S, GDN_WIDTH), bf16),
        grid=(B, GDN_HEADS),
        in_specs=[smem, smem, col(COL_GQ), col(COL_GK), col(COL_GV), col(COL_GZ),
                  pl.BlockSpec((1, S, LANES), lambda b, h: (b, 0, 0)),
                  cw(0), cw(GDN_WIDTH), cw(2 * GDN_WIDTH),
                  pl.BlockSpec((1, hb), lambda b, h: (0, 0))],
        out_specs=pl.BlockSpec((1, S, hb), lambda b, h: (b, 0, h)),
        scratch_shapes=[pltpu.VMEM((S, hb), f32), pltpu.VMEM((S, hb), f32), pltpu.VMEM((S, hb), f32),
                        pltpu.VMEM((3, S + 2 * CONV_PAD, hb), f32),
                        pltpu.VMEM((2, nsb * 2 * SB, hb), bf16),
                        pltpu.VMEM((2, nsb * SB, hb), f32),
                        pltpu.VMEM((2, nsb * 8, hb), f32),
                        pltpu.VMEM((S, hb), f32)],
        compiler_params=_cparams(("parallel", "arbitrary")),
        name="gdn",
    )(a_log, dt_bias, proj3, proj3, proj3, proj3, ab3, conv_w, conv_w, conv_w, norm_w)


N_BIAS_TILES = 6
KV_TILE = 2 * LANES


def _bias_kernel(rb_ref, bucket_ref, lp_ref, bias_ref, lam_ref):
    bucket = bucket_ref[...]
    for hh in range(DIFF_HEADS):
        acc = jnp.zeros(bucket.shape, f32)
        for bk in range(NUM_BUCKETS):
            acc = jnp.where(bucket == bk, rb_ref[bk, hh], acc)
        bias_ref[hh] = acc
    lp = lp_ref[...]
    lam = (jnp.exp(jnp.sum(lp[0:1] * lp[1:2], axis=-1, keepdims=True))
           - jnp.exp(jnp.sum(lp[2:3] * lp[3:4], axis=-1, keepdims=True)) + LAMBDA_INIT)
    lam_ref[...] = jnp.broadcast_to(lam, lam_ref.shape)


def _t5_bucket(rel):
    nb = NUM_BUCKETS // 2
    max_exact = nb // 2
    ret = jnp.where(rel > 0, nb, 0)
    n = jnp.abs(rel)
    nf = jnp.maximum(n, 1).astype(jnp.float32)
    large = max_exact + (jnp.log(nf / max_exact) / math.log(MAX_DISTANCE / max_exact)
                         * (nb - max_exact)).astype(jnp.int32)
    large = jnp.minimum(large, nb - 1)
    return ret + jnp.where(n < max_exact, n, large)


def _bias_tiles(rel_bias, lam_params):
    delta = jnp.arange(N_BIAS_TILES, dtype=jnp.int32)[:, None, None] - 3
    rel = (LANES * delta + jnp.arange(KV_TILE, dtype=jnp.int32)[None, None, :]
           - jnp.arange(LANES, dtype=jnp.int32)[None, :, None])
    bucket = _t5_bucket(rel).astype(jnp.int32)
    return pl.pallas_call(
        _bias_kernel,
        out_shape=(jax.ShapeDtypeStruct((DIFF_HEADS, N_BIAS_TILES, LANES, KV_TILE), f32),
                   jax.ShapeDtypeStruct((8, LANES), f32)),
        in_specs=[pl.BlockSpec(memory_space=pltpu.SMEM),
                  pl.BlockSpec(memory_space=pltpu.VMEM),
                  pl.BlockSpec(memory_space=pltpu.VMEM)],
        out_specs=(pl.BlockSpec(memory_space=pltpu.VMEM), pl.BlockSpec(memory_space=pltpu.VMEM)),
        name="rel_bias",
    )(rel_bias, bucket, lam_params)


def _diff_kernel(q_ref, k_ref, v_ref, bias_ref, lam_ref, nw_ref, o_ref, s_ref, e_ref, *, S, QB):
    qi = pl.program_id(2)
    n_tiles = S // KV_TILE
    nsub = QB // LANES
    dh = DIFF_HEAD_DIM
    q = (q_ref[0].astype(f32) * (dh ** -0.5)).astype(bf16)

    def tile(jt):
        return slice(jt * KV_TILE, (jt + 1) * KV_TILE)

    def score_tile(m, jt, mx):
        s = lax.dot_general(q[:, m * dh:(m + 1) * dh], k_ref[0, tile(jt), m * dh:(m + 1) * dh], NT_DIMS,
                            preferred_element_type=f32)
        parts = []
        for r in range(nsub):
            idx = jnp.clip(2 * jt - (qi * nsub + r), -3, 2) + 3
            parts.append(s[r * LANES:(r + 1) * LANES, :] + bias_ref[0, idx])
        s = jnp.concatenate(parts, axis=0) if nsub > 1 else parts[0]
        s_ref[m, :, tile(jt)] = s
        return jnp.maximum(mx, jnp.maximum(s[:, :LANES], s[:, LANES:]))

    def exp_tile(m, jt, row_max, den):
        e = jnp.exp(s_ref[m, :, tile(jt)] - row_max)
        e_ref[m, :, tile(jt)] = e.astype(bf16)
        return den + (e[:, :LANES] + e[:, LANES:])

    def pv_tile(m, jt, acc):
        return acc + jnp.dot(e_ref[m, :, tile(jt)], v_ref[0, tile(jt), :], preferred_element_type=f32)

    neg = jnp.full((QB, LANES), -jnp.inf, f32)
    zero = jnp.zeros((QB, LANES), f32)
    mx0 = neg
    for jt in range(n_tiles):
        mx0 = score_tile(0, jt, mx0)
    max0 = jnp.max(mx0, axis=-1, keepdims=True)
    mx1, den0 = neg, zero
    for jt in range(n_tiles):
        mx1 = score_tile(1, jt, mx1)
        den0 = exp_tile(0, jt, max0, den0)
    max1 = jnp.max(mx1, axis=-1, keepdims=True)
    pv0, den1 = jnp.zeros((QB, 2 * dh), f32), zero
    for jt in range(n_tiles):
        pv0 = pv_tile(0, jt, pv0)
        den1 = exp_tile(1, jt, max1, den1)
    pv1 = jnp.zeros((QB, 2 * dh), f32)
    for jt in range(n_tiles):
        pv1 = pv_tile(1, jt, pv1)
    outs = [pv0 / jnp.sum(den0, axis=-1, keepdims=True), pv1 / jnp.sum(den1, axis=-1, keepdims=True)]
    lam = lam_ref[0:1, 0:1]
    o = outs[0] - lam * outs[1]
    o = o * lax.rsqrt(jnp.mean(o * o, axis=-1, keepdims=True) + RMS_EPS) * nw_ref[...]
    o_ref[0] = (o * (1.0 - LAMBDA_INIT)).astype(bf16)


def _diff_attention(proj3, bias_tiles, lam, norm_w, *, qb):
    B, S, _ = proj3.shape
    w = 2 * DIFF_HEAD_DIM
    return pl.pallas_call(
        functools.partial(_diff_kernel, S=S, QB=qb),
        out_shape=jax.ShapeDtypeStruct((B, S, DIFF_HEADS * w), bf16),
        grid=(B, DIFF_HEADS, S // qb),
        in_specs=[pl.BlockSpec((1, qb, w), lambda b, h, i: (b, i, COL_DQ // w + h)),
                  pl.BlockSpec((1, S, w), lambda b, h, i: (b, 0, COL_DK // w + h)),
                  pl.BlockSpec((1, S, w), lambda b, h, i: (b, 0, COL_DV // w + h)),
                  pl.BlockSpec((1, N_BIAS_TILES, LANES, KV_TILE), lambda b, h, i: (h, 0, 0, 0)),
                  pl.BlockSpec((8, LANES), lambda b, h, i: (0, 0)),
                  pl.BlockSpec((1, w), lambda b, h, i: (0, 0))],
        out_specs=pl.BlockSpec((1, qb, w), lambda b, h, i: (b, i, h)),
        scratch_shapes=[pltpu.VMEM((2, qb, S), f32), pltpu.VMEM((2, qb, S), bf16)],
        compiler_params=_cparams(("parallel", "parallel", "arbitrary")),
        name="diff_attn",
    )(proj3, proj3, proj3, bias_tiles, lam, norm_w)


def _cross_kernel(q_ref, kv_ref, o_ref):
    w = CROSS_HEAD_DIM
    heads = range(CROSS_HEADS)
    s = [lax.dot_general(q_ref[0, :, h * w:(h + 1) * w], kv_ref[0, :, h * w:(h + 1) * w], NT_DIMS,
                         preferred_element_type=f32) * (w ** -0.5) for h in heads]
    e = [jnp.exp(s[h] - jnp.max(s[h], axis=-1, keepdims=True)) for h in heads]
    p = [(e[h] / jnp.sum(e[h], axis=-1, keepdims=True)).astype(bf16) for h in heads]
    for h in heads:
        o_ref[0, :, h * w:(h + 1) * w] = jnp.dot(
            p[h], kv_ref[0, :, (CROSS_HEADS + h) * w:(CROSS_HEADS + h + 1) * w],
            preferred_element_type=f32).astype(bf16)


def _cross_attention(proj3, kv3, *, qb):
    B, S, _ = proj3.shape
    M = kv3.shape[1]
    cw = CROSS_HEADS * CROSS_HEAD_DIM
    return pl.pallas_call(
        _cross_kernel,
        out_shape=jax.ShapeDtypeStruct((B, S, cw), bf16),
        grid=(B, S // qb),
        in_specs=[pl.BlockSpec((1, qb, cw), lambda b, i: (b, i, COL_CQ // cw)),
                  pl.BlockSpec((1, M, 2 * cw), lambda b, i: (b, 0, 0))],
        out_specs=pl.BlockSpec((1, qb, cw), lambda b, i: (b, i, 0)),
        compiler_params=_cparams(("parallel", "arbitrary")),
        name="cross_attn",
    )(proj3, kv3)


def _merge_kernel(yg_ref, yd_ref, yc_ref, g0_ref, g1_ref, g2_ref, wg_ref, wd_ref, wc_ref, o_ref):
    m = g0_ref[...].astype(f32) * jnp.dot(yg_ref[...], wg_ref[...], preferred_element_type=f32)
    m = m + g1_ref[...].astype(f32) * jnp.dot(yd_ref[...], wd_ref[...], preferred_element_type=f32)
    m = m + g2_ref[...].astype(f32) * jnp.dot(yc_ref[...], wc_ref[...], preferred_element_type=f32)
    o_ref[...] = m.astype(bf16)


def _merge(yg, yd, yc, proj2, wg, wd, wc, *, tm):
    T = yg.shape[0]
    gate_blk = COL_GATE // D_MODEL

    def yspec():
        return pl.BlockSpec((tm, GDN_WIDTH), lambda i: (i, 0))

    def gspec(n):
        return pl.BlockSpec((tm, D_MODEL), lambda i, n=n: (i, gate_blk + n))

    def wspec():
        return pl.BlockSpec((GDN_WIDTH, D_MODEL), lambda i: (0, 0), pipeline_mode=pl.Buffered(1))

    return pl.pallas_call(
        _merge_kernel,
        out_shape=jax.ShapeDtypeStruct((T, D_MODEL), bf16),
        grid=(T // tm,),
        in_specs=[yspec(), yspec(), yspec(), gspec(0), gspec(1), gspec(2), wspec(), wspec(), wspec()],
        out_specs=pl.BlockSpec((tm, D_MODEL), lambda i: (i, 0)),
        compiler_params=_cparams(("parallel",)),
        name="merge",
    )(yg, yd, yc, proj2, proj2, proj2, wg, wd, wc)


def _outproj_kernel(m_ref, x_ref, w_ref, g_ref, b_ref, o_ref):
    y = DEEPNORM_ALPHA * x_ref[...] + jnp.dot(m_ref[...], w_ref[...], preferred_element_type=f32)
    o_ref[...] = _layer_norm(y, g_ref[...], b_ref[...])


def _outproj_ln(merged, x2, w_out, g, b, *, tm):
    T = x2.shape[0]
    return pl.pallas_call(
        _outproj_kernel,
        out_shape=jax.ShapeDtypeStruct((T, D_MODEL), f32),
        grid=(T // tm,),
        in_specs=[pl.BlockSpec((tm, D_MODEL), lambda i: (i, 0)),
                  pl.BlockSpec((tm, D_MODEL), lambda i: (i, 0)),
                  pl.BlockSpec((D_MODEL, D_MODEL), lambda i: (0, 0), pipeline_mode=pl.Buffered(1)),
                  pl.BlockSpec((1, D_MODEL), lambda i: (0, 0)),
                  pl.BlockSpec((1, D_MODEL), lambda i: (0, 0))],
        out_specs=pl.BlockSpec((tm, D_MODEL), lambda i: (i, 0)),
        compiler_params=_cparams(("parallel",)),
        name="outproj_ln",
    )(merged, x2, w_out, g, b)


HALO = 16
FFN_TN = 512
FFN_SUB = 128
FFN_ROWS = 128


def _ffn_kernel(xm_ref, xp_ref, xn_ref, wg_ref, wv_ref, cg_ref, cv_ref, wdn_ref, g_ref, b_ref, o_ref,
                xh_ref, up_ref, h_ref, *, tm, tn, tiles_per_seq):
    i = pl.program_id(0)
    n = pl.program_id(1)

    @pl.when(n == 0)
    def _():
        pos = i % tiles_per_seq
        keep_prev = (pos != 0).astype(f32)
        keep_next = (pos != tiles_per_seq - 1).astype(f32)
        xh_ref[0:HALO, :] = (xp_ref[...] * keep_prev).astype(bf16)
        xh_ref[HALO:HALO + tm, :] = xm_ref[...].astype(bf16)
        xh_ref[HALO + tm:2 * HALO + tm, :] = (xn_ref[...] * keep_next).astype(bf16)
        o_ref[...] = DEEPNORM_ALPHA * xm_ref[...]

    taps = (cg_ref[...], cv_ref[...])
    planes = FFN_SUB // LANES

    def up_dot(c):
        cols = slice(c * FFN_SUB, (c + 1) * FFN_SUB)
        w = jnp.concatenate([wg_ref[:, cols], wv_ref[:, cols]], axis=1)
        up = jnp.dot(xh_ref[...], w, preferred_element_type=f32)
        for q in range(2 * planes):
            up_ref[c % 2, q] = up[:, q * LANES:(q + 1) * LANES]

    def act(c):
        for p in range(planes):
            c0 = c * FFN_SUB + p * LANES
            for r in range(tm // FFN_ROWS):
                r0 = r * FFN_ROWS
                conv = [sum(up_ref[c % 2, gv * planes + p, r0 + HALO - 1 + j:r0 + HALO - 1 + j + FFN_ROWS, :]
                            * taps[gv][j:j + 1, c0:c0 + LANES] for j in range(3)) for gv in range(2)]
                h_ref[r0:r0 + FFN_ROWS, c0:c0 + LANES] = (_silu(conv[0]) * conv[1]).astype(bf16)

    nsub = tn // FFN_SUB
    up_dot(0)
    for c in range(nsub):
        if c + 1 < nsub:
            up_dot(c + 1)
        act(c)
    o_ref[...] += jnp.dot(h_ref[...], wdn_ref[...], preferred_element_type=f32)

    @pl.when(n == pl.num_programs(1) - 1)
    def _():
        o_ref[...] = _layer_norm(o_ref[...], g_ref[...], b_ref[...])


def _ffn_ln(x1, w_up_p, conv_p, w_down_p, g, b, *, S, tm, tn):
    T = x1.shape[0]
    nf = D_FF_PAD // tn
    hpt = tm // HALO
    last_halo = T // HALO - 1
    return pl.pallas_call(
        functools.partial(_ffn_kernel, tm=tm, tn=tn, tiles_per_seq=S // tm),
        out_shape=jax.ShapeDtypeStruct((T, D_MODEL), f32),
        grid=(T // tm, nf),
        in_specs=[pl.BlockSpec((tm, D_MODEL), lambda i, n: (i, 0)),
                  pl.BlockSpec((HALO, D_MODEL), lambda i, n: (jnp.maximum(i * hpt - 1, 0), 0)),
                  pl.BlockSpec((HALO, D_MODEL), lambda i, n: (jnp.minimum((i + 1) * hpt, last_halo), 0)),
                  pl.BlockSpec((D_MODEL, tn), lambda i, n: (0, n)),
                  pl.BlockSpec((D_MODEL, tn), lambda i, n: (0, nf + n)),
                  pl.BlockSpec((3, tn), lambda i, n: (0, n)),
                  pl.BlockSpec((3, tn), lambda i, n: (0, nf + n)),
                  pl.BlockSpec((tn, D_MODEL), lambda i, n: (n, 0)),
                  pl.BlockSpec((1, D_MODEL), lambda i, n: (0, 0)),
                  pl.BlockSpec((1, D_MODEL), lambda i, n: (0, 0))],
        out_specs=pl.BlockSpec((tm, D_MODEL), lambda i, n: (i, 0)),
        scratch_shapes=[pltpu.VMEM((tm + 2 * HALO, D_MODEL), bf16),
                        pltpu.VMEM((2, 2 * FFN_SUB // LANES, tm + 2 * HALO, LANES), f32),
                        pltpu.VMEM((tm, tn), bf16)],
        compiler_params=_cparams(("parallel", "arbitrary")),
        name="ffn_ln",
    )(x1, x1, x1, w_up_p, w_up_p, conv_p, conv_p, w_down_p, g, b)


def _pick(n, prefs):
    for p in prefs:
        if n % p == 0:
            return p
    raise ValueError(f"no tile in {prefs} divides {n}")


def _trunk(x, mem, wts):
    B, S, _ = x.shape
    T = B * S
    assert S % CONV_ROWS == 0 and S % KV_TILE == 0 and (S // SB) % GDN_UNROLL == 0
    x2 = x.reshape(T, D_MODEL)

    proj2, ab2 = _inproj(x2, wts["w_all"], wts["b_all"], wts["w_ab"], tm=_pick(T, (1024, 512, 256)), tn=1024)
    proj3 = proj2.reshape(B, S, PROJ_COLS)
    ab3 = ab2.reshape(B, S, LANES)

    y_gdn = _gdn(proj3, ab3, wts["gdn_conv"], wts["a_log"], wts["dt_bias"], wts["gdn_norm_w"])
    y_diff = _diff_attention(proj3, wts["bias_tiles"], wts["lam"], wts["diff_norm_w"], qb=_pick(S, (512, 256)))
    M = mem.shape[1]
    kv = _matmul(mem.reshape(B * M, D_MODEL), wts["w_mem_kv"], tm=_pick(B * M, (512, 256)))
    y_cross = _cross_attention(proj3, kv.reshape(B, M, 2 * CROSS_HEADS * CROSS_HEAD_DIM), qb=_pick(S, (1024, 512, 256)))

    tm = _pick(S, (512, 256))
    merged = _merge(y_gdn.reshape(T, -1), y_diff.reshape(T, -1), y_cross.reshape(T, -1), proj2,
                    wts["w_bg"], wts["w_bd"], wts["w_bc"], tm=tm)
    x1 = _outproj_ln(merged, x2, wts["w_out"], wts["ln1_g"], wts["ln1_b"], tm=tm)
    y = _ffn_ln(x1, wts["w_up_p"], wts["ffn_conv_p"], wts["w_down_p"], wts["ln2_g"], wts["ln2_b"],
                S=S, tm=tm, tn=FFN_TN)
    return y.reshape(B, S, D_MODEL)


def _prep_weights(rel_bias, w_in, gdn_conv, gdn_a_log, gdn_dt_bias, gdn_norm_w, diff_lambda, diff_norm_w,
                  w_mem_kv, w_gate, b_gate, w_branch_gdn, w_branch_diff, w_branch_cross, w_out,
                  ln1_g, ln1_b, w_up, ffn_conv, w_down, ln2_g, ln2_b):
    l = 0
    wi = w_in[l]
    a0 = 4 * GDN_WIDTH
    a1 = a0 + 4 * GDN_HEADS
    w_lin = jnp.concatenate([wi[:, :a0], wi[:, a1:]], axis=1)
    w_all = jnp.concatenate([w_lin, w_gate[l]], axis=1).astype(bf16)
    b_all = jnp.concatenate([jnp.zeros((PROJ_LIN,), f32), b_gate[l].astype(f32)])[None, :]
    w_ab = jnp.pad(wi[:, a0:a1], ((0, 0), (0, LANES - 4 * GDN_HEADS))).astype(bf16)

    padc = D_FF_PAD - D_FF

    def pad_halves(a):
        zeros = jnp.zeros((a.shape[0], padc), a.dtype)
        return jnp.concatenate([a[:, :D_FF], zeros, a[:, D_FF:], zeros], axis=1)

    bias_tiles, lam = _bias_tiles(rel_bias.astype(f32), diff_lambda[l].astype(f32))
    return dict(
        w_all=w_all, b_all=b_all, w_ab=w_ab,
        gdn_conv=gdn_conv[l].astype(f32), a_log=gdn_a_log[l].astype(f32), dt_bias=gdn_dt_bias[l].astype(f32),
        gdn_norm_w=gdn_norm_w[l].astype(f32)[None, :],
        bias_tiles=bias_tiles, lam=lam, diff_norm_w=diff_norm_w[l].astype(f32)[None, :],
        w_mem_kv=w_mem_kv[l].astype(bf16),
        w_bg=w_branch_gdn[l].astype(bf16), w_bd=w_branch_diff[l].astype(bf16), w_bc=w_branch_cross[l].astype(bf16),
        w_out=w_out[l].astype(bf16),
        ln1_g=ln1_g[l].astype(f32)[None, :], ln1_b=ln1_b[l].astype(f32)[None, :],
        w_up_p=pad_halves(w_up[l].astype(bf16)), ffn_conv_p=pad_halves(ffn_conv[l].astype(f32)),
        w_down_p=jnp.pad(w_down[l], ((0, padc), (0, 0))).astype(bf16),
        ln2_g=ln2_g[l].astype(f32)[None, :], ln2_b=ln2_b[l].astype(f32)[None, :],
    )


def kernel(x_prompt, x_sample, mem_prompt, mem_sample, rel_bias, w_in, gdn_conv, gdn_a_log, gdn_dt_bias, gdn_norm_w, diff_lambda, diff_norm_w, w_mem_kv, w_gate, b_gate, w_branch_gdn, w_branch_diff, w_branch_cross, w_out, ln1_g, ln1_b, w_up, ffn_conv, w_down, ln2_g, ln2_b):
    wts = _prep_weights(rel_bias, w_in, gdn_conv, gdn_a_log, gdn_dt_bias, gdn_norm_w, diff_lambda, diff_norm_w,
                        w_mem_kv, w_gate, b_gate, w_branch_gdn, w_branch_diff, w_branch_cross, w_out,
                        ln1_g, ln1_b, w_up, ffn_conv, w_down, ln2_g, ln2_b)
    return (_trunk(x_prompt, mem_prompt, wts), _trunk(x_sample, mem_sample, wts))
```

```python
import functools
import math

import jax
import jax.numpy as jnp
import numpy as np
from jax import lax
from jax.experimental import pallas as pl
from jax.experimental.pallas import tpu as pltpu

f32 = jnp.float32
bf16 = jnp.bfloat16

D_MODEL = 2048
GDN_HEADS = 8
GDN_HEAD_DIM = 128
GDN_WIDTH = 1024
GDN_CONV = 5
GDN_CHUNK = 64
DIFF_HEADS = 4
DIFF_HEAD_DIM = 128
CROSS_HEADS = 4
CROSS_HEAD_DIM = 256
N_BRANCH = 3
D_FF = 5504
NUM_BUCKETS = 32
MAX_DISTANCE = 128
LN_EPS = 1e-5
RMS_EPS = 1e-6
L2_EPS = 1e-6
DEPTH = 1
DEEPNORM_ALPHA = (2 * DEPTH) ** 0.25
LAMBDA_INIT = 0.8 - 0.6 * math.exp(-0.3 * 0)

LANES = 128
MXU_WIDTH = 256
VMEM_LIMIT_BYTES = 56 * 1024 * 1024

PROJ_LIN = 8192
COL_GQ, COL_GK, COL_GV, COL_GZ = 0, 1024, 2048, 3072
COL_DQ, COL_DK, COL_DV, COL_CQ = 4096, 5120, 6144, 7168
COL_GATE = PROJ_LIN
PROJ_COLS = PROJ_LIN + N_BRANCH * D_MODEL
D_FF_PAD = 5632

NT_DIMS = (((1,), (1,)), ((), ()))


def _cparams(sem, vmem=VMEM_LIMIT_BYTES):
    return pltpu.CompilerParams(dimension_semantics=sem, vmem_limit_bytes=vmem)


def _sigmoid(x):
    return 1.0 / (1.0 + jnp.exp(-x))


def _silu(x):
    return x * _sigmoid(x)


def _layer_norm(y, g, b):
    mu = jnp.mean(y, axis=-1, keepdims=True)
    yc = y - mu
    var = jnp.mean(yc * yc, axis=-1, keepdims=True)
    return yc * lax.rsqrt(var + LN_EPS) * g + b


def _bdot(a, b):
    return jnp.dot(a.astype(bf16), b.astype(bf16), preferred_element_type=f32)


def _inproj_kernel(x_ref, w_ref, b_ref, wab_ref, o_ref, ab_ref, xb_ref, *, n_lin_tiles):
    j = pl.program_id(1)

    @pl.when(j == 0)
    def _():
        xb = x_ref[...].astype(bf16)
        xb_ref[...] = xb
        ab_ref[...] = jnp.dot(xb, wab_ref[...], preferred_element_type=f32)

    acc = jnp.dot(xb_ref[...], w_ref[...], preferred_element_type=f32)
    o_ref[...] = jnp.where(j >= n_lin_tiles, _sigmoid(acc + b_ref[...]), acc).astype(bf16)


def _inproj(x2, w_all, b_all, w_ab, *, tm, tn):
    T = x2.shape[0]
    n_lin_tiles = PROJ_LIN // tn
    return pl.pallas_call(
        functools.partial(_inproj_kernel, n_lin_tiles=n_lin_tiles),
        out_shape=(jax.ShapeDtypeStruct((T, PROJ_COLS), bf16),
                   jax.ShapeDtypeStruct((T, LANES), f32)),
        grid=(T // tm, PROJ_COLS // tn),
        in_specs=[pl.BlockSpec((tm, D_MODEL), lambda i, j: (i, 0)),
                  pl.BlockSpec((D_MODEL, tn), lambda i, j: (0, j)),
                  pl.BlockSpec((1, tn), lambda i, j: (0, j)),
                  pl.BlockSpec((D_MODEL, LANES), lambda i, j: (0, 0))],
        out_specs=(pl.BlockSpec((tm, tn), lambda i, j: (i, j)),
                   pl.BlockSpec((tm, LANES), lambda i, j: (i, 0))),
        scratch_shapes=[pltpu.VMEM((tm, D_MODEL), bf16)],
        compiler_params=_cparams(("parallel", "arbitrary")),
        name="inproj",
    )(x2, w_all, b_all, w_ab)


def _mm_kernel(x_ref, w_ref, o_ref):
    o_ref[...] = jnp.dot(x_ref[...].astype(bf16), w_ref[...], preferred_element_type=f32).astype(o_ref.dtype)


def _matmul(x2, w, *, tm, out_dtype=bf16):
    T, K = x2.shape
    N = w.shape[1]
    return pl.pallas_call(
        _mm_kernel,
        out_shape=jax.ShapeDtypeStruct((T, N), out_dtype),
        grid=(T // tm,),
        in_specs=[pl.BlockSpec((tm, K), lambda i: (i, 0)),
                  pl.BlockSpec((K, N), lambda i: (0, 0), pipeline_mode=pl.Buffered(1))],
        out_specs=pl.BlockSpec((tm, N), lambda i: (i, 0)),
        compiler_params=_cparams(("parallel",)),
        name="mem_kv",
    )(x2, w)


SB = 2 * GDN_CHUNK
CONV_ROWS = 256
CONV_PAD = 8
GDN_UNROLL = 8
INV_BASE = 8


def _split_bf16(x):
    hi = x.astype(bf16)
    lo = (x - hi.astype(f32)).astype(bf16)
    return hi, lo


def _gdn_kernel(alog_ref, dtb_ref, q_ref, k_ref, v_ref, z_ref, ab_ref, cq_ref, ck_ref, cv_ref,
                nw_ref, o_ref, qn_ref, kn_ref, vn_ref, xp_ref, lhs_ref, n_ref, egl_ref, oacc_ref, *, S, unroll):
    h = pl.program_id(1)
    nsb = S // SB
    hd = GDN_HEAD_DIM

    zpad = jnp.zeros((CONV_PAD, hd), f32)
    streams = ((q_ref, cq_ref, qn_ref, "q"), (k_ref, ck_ref, kn_ref, "k"), (v_ref, cv_ref, vn_ref, "v"))
    for si, (src_ref, _, _, _) in enumerate(streams):
        xp_ref[si, 0:CONV_PAD, :] = zpad
        xp_ref[si, S + CONV_PAD:S + 2 * CONV_PAD, :] = zpad
        xp_ref[si, CONV_PAD:S + CONV_PAD, :] = src_ref[0].astype(f32)
    taps = [w_ref[...] for _, w_ref, _, _ in streams]

    def conv_body(r, c):
        r0 = pl.multiple_of(r * CONV_ROWS, CONV_ROWS)
        for si, (_, _, dst_ref, mode) in enumerate(streams):
            w = taps[si]
            acc = jnp.zeros((CONV_ROWS, hd), f32)
            for j in range(GDN_CONV):
                acc = acc + xp_ref[si, pl.ds(r0 + CONV_PAD - GDN_CONV // 2 + j, CONV_ROWS), :] * w[j:j + 1, :]
            y = _silu(acc)
            if mode != "v":
                y = y * lax.rsqrt(jnp.sum(y * y, axis=-1, keepdims=True) + L2_EPS)
            if mode == "q":
                y = y * (hd ** -0.5)
            dst_ref[pl.ds(r0, CONV_ROWS), :] = y
        return c

    lax.fori_loop(0, S // CONV_ROWS, conv_body, 0)

    ri = lax.broadcasted_iota(jnp.int32, (SB, SB), 0)
    ci = lax.broadcasted_iota(jnp.int32, (SB, SB), 1)

    def same(blk):
        return (ri // blk) == (ci // blk)

    same_chunk = same(GDN_CHUNK)
    incl = (same_chunk & (ri >= ci), same_chunk & (ri <= ci))
    strict = (same_chunk & (ri > ci), same_chunk & (ri < ci))
    same_base = same(INV_BASE)
    merge_masks = []
    blk = INV_BASE
    while blk < GDN_CHUNK:
        merge_masks.append(same(2 * blk) & jnp.logical_not(same(blk)))
        blk *= 2
    eye = (ri == ci).astype(f32)
    tri2 = [jnp.concatenate([incl[d].astype(bf16)] * 2, axis=1) for d in range(2)]
    r4 = lax.broadcasted_iota(jnp.int32, (2 * SB, 4 * hd), 0) % SB
    c4 = lax.broadcasted_iota(jnp.int32, (2 * SB, 4 * hd), 1) // hd
    sel = (r4 == (c4 * GDN_HEADS + h)).astype(bf16)
    rh = lax.broadcasted_iota(jnp.int32, (SB, 4 * hd), 0) < GDN_CHUNK
    ch = lax.broadcasted_iota(jnp.int32, (SB, 4 * hd), 1) < 2 * hd
    half_mask = rh == ch

    ends = unroll // 2
    n_groups = nsb // unroll

    def chunk_local(i, rec=()):
        rec = list(rec)

        def run_rec():
            if rec:
                rec.pop(0)()

        chains = []
        for u in range(unroll):
            sb = i * ends + u if u < ends else nsb - (i + 1) * ends + (u - ends)
            rows = pl.ds(pl.multiple_of(sb * SB, SB), SB)
            kn = kn_ref[rows, :]
            qn = qn_ref[rows, :]
            vn = vn_ref[rows, :]
            knb = kn.astype(bf16)
            kq = lax.dot_general(jnp.concatenate([knb, qn.astype(bf16)], axis=0), knb, NT_DIMS,
                                 preferred_element_type=f32)
            ab_hi, ab_lo = _split_bf16(ab_ref[0, rows, :])
            absel = jnp.dot(jnp.concatenate([ab_hi, ab_lo], axis=1), sel, preferred_element_type=f32)
            for d in range(2):
                chains.append(dict(sb=sb, rows=rows, d=d, kn=kn, qn=qn, vn=vn, kk=kq[:SB], qk=kq[SB:],
                                   a=absel[:, d * hd:(d + 1) * hd], b=absel[:, (2 + d) * hd:(3 + d) * hd]))
        for t in chains:
            d = t["d"]
            xs = t["a"] + dtb_ref[d, h]
            softplus = jnp.maximum(xs, 0.0) + jnp.log1p(jnp.exp(-jnp.abs(xs)))
            g = -jnp.exp(jnp.full((1, hd), alog_ref[d, h], f32)) * softplus
            t["beta"] = _sigmoid(t["b"])
            g_hi, g_lo = _split_bf16(g)
            t["gc"] = jnp.dot(tri2[d], jnp.concatenate([g_hi, g_lo], axis=0),
                              preferred_element_type=f32)
        run_rec()
        for t in chains:
            d, gc, beta = t["d"], t["gc"], t["beta"]
            diff = gc - gc.T
            t["decay"] = jnp.where(incl[d], jnp.exp(jnp.where(incl[d], diff, 0.0)), 0.0)
            t["l"] = jnp.where(strict[d], beta * t["kk"] * t["decay"], 0.0)
            t["egc"] = jnp.exp(gc)
            t["rhs"] = jnp.concatenate([beta * t["vn"], (beta * t["egc"]) * t["kn"]], axis=1)
            t["ld"] = jnp.where(same_base, t["l"], 0.0)
            t["ld2"] = _bdot(t["ld"], t["ld"])
        run_rec()
        for t in chains:
            iml = eye - t["ld"]
            t["x"] = iml + _bdot(iml, t["ld2"])
            t["ld4"] = _bdot(t["ld2"], t["ld2"])
        run_rec()
        for t in chains:
            t["t"] = t["x"] + _bdot(t["x"], t["ld4"])
        run_rec()
        for mask in merge_masks:
            for t in chains:
                t["w"] = _bdot(t["t"], jnp.where(mask, t["l"], 0.0))
            run_rec()
            for t in chains:
                t["t"] = t["t"] - _bdot(t["w"], t["t"])
            run_rec()
        for t in chains:
            t["sol"] = _bdot(t["t"], t["rhs"])
        while rec:
            run_rec()
        for t in chains:
            d, gc, sol = t["d"], t["gc"], t["sol"]
            w = jnp.concatenate([sol[:, hd:], sol[:, :hd]], axis=1).astype(bf16)
            qkd = jnp.where(incl[d], t["qk"] * t["decay"], 0.0).astype(bf16)
            t["qw"] = jnp.dot(qkd, w, preferred_element_type=f32)
            last = (GDN_CHUNK - 1) if d == 0 else 0
            gl = jnp.concatenate(
                [jnp.broadcast_to(gc[last:last + 1, :], (GDN_CHUNK, hd)),
                 jnp.broadcast_to(gc[GDN_CHUNK + last:GDN_CHUNK + last + 1, :], (GDN_CHUNK, hd))], axis=0)
            kdt = (t["kn"] * jnp.exp(gl - gc)).T.astype(bf16)
            wpair = jnp.where(half_mask, jnp.concatenate([w, w], axis=1), jnp.zeros((SB, 4 * hd), bf16))
            t["kw"] = jnp.dot(kdt, wpair, preferred_element_type=f32)
            t["egl"] = jnp.exp(gl)
        for t in chains:
            d = t["d"]
            f_, s_ = (0, 1) if d == 0 else (1, 0)
            kw = t["kw"]
            t["qp"] = t["qn"] * t["egc"] - t["qw"][:, :hd]
            a = [-kw[:, 2 * hf * hd:(2 * hf + 1) * hd] for hf in range(2)]
            nn = [kw[:, (2 * hf + 1) * hd:(2 * hf + 2) * hd] for hf in range(2)]
            qs = t["qp"][s_ * GDN_CHUNK:(s_ + 1) * GDN_CHUNK, :]
            t["am"], t["nm"] = a, nn
            t["xc"] = _bdot(jnp.concatenate([a[s_], qs], axis=0), jnp.concatenate([a[f_], nn[f_]], axis=1))
        for t in chains:
            d, sb = t["d"], t["sb"]
            f_, s_ = (0, 1) if d == 0 else (1, 0)
            a, nn, x, qp = t["am"], t["nm"], t["xc"], t["qp"]
            dec = [t["egl"][hf * GDN_CHUNK:hf * GDN_CHUNK + 1, :] for hf in range(2)]
            a2 = dec[s_] * a[f_] + dec[f_] * a[s_] + x[:hd, :hd]
            n2 = dec[s_] * nn[f_] + x[:hd, hd:] + nn[s_]
            q_f = qp[f_ * GDN_CHUNK:(f_ + 1) * GDN_CHUNK, :]
            q_s = dec[f_] * qp[s_ * GDN_CHUNK:(s_ + 1) * GDN_CHUNK, :] + x[hd:, :hd]
            q2 = jnp.concatenate([q_f, q_s] if d == 0 else [q_s, q_f], axis=0)
            zero = jnp.zeros((GDN_CHUNK, hd), f32)
            t["o"] = t["qw"][:, hd:] + jnp.concatenate([zero, x[hd:, hd:]] if d == 0 else [x[hd:, hd:], zero], axis=0)
            l0 = pl.multiple_of(sb * 2 * SB, 2 * SB)
            lhs_ref[d, pl.ds(l0, SB), :] = a2.astype(bf16)
            lhs_ref[d, pl.ds(l0 + SB, SB), :] = q2.astype(bf16)
            n_ref[d, pl.ds(pl.multiple_of(sb * SB, SB), SB), :] = n2
            egl_ref[d, pl.ds(pl.multiple_of(sb * 8, 8), 8), :] = jnp.broadcast_to(dec[0] * dec[1], (8, hd))
        for tf, tb in zip(chains[0::2], chains[1::2]):
            oacc_ref[tf["rows"], :] = tf["o"] + tb["o"]

    def sb_step(d, sbi, state):
        lhs = lhs_ref[d, pl.ds(pl.multiple_of(sbi * 2 * SB, 2 * SB), 2 * SB), :]
        pop = jnp.dot(lhs, state.astype(bf16), preferred_element_type=f32)
        rows = pl.ds(pl.multiple_of(sbi * SB, SB), SB)
        oacc_ref[rows, :] += pop[SB:, :]
        dec = egl_ref[d, pl.ds(pl.multiple_of(sbi * 8, 8), 8), :][0:1, :]
        return dec * state + pop[:SB, :] + n_ref[d, rows, :]

    def rec_steps(g, states):
        def step(j):
            def run():
                sf, sbk = states[0]
                t = g * ends + j
                states[0] = (sb_step(0, t, sf), sb_step(1, nsb - 1 - t, sbk))
            return run
        return [step(j) for j in range(ends)]

    z0 = jnp.zeros((hd, hd), f32)
    chunk_local(0)

    def piped(i, carry):
        states = [carry]
        chunk_local(i, rec_steps(i - 1, states))
        return states[0]

    carry = lax.fori_loop(1, n_groups, piped, (z0, z0))
    states = [carry]
    for run in rec_steps(n_groups - 1, states):
        run()

    nw = nw_ref[...]

    def finish(sbi):
        rows = pl.ds(pl.multiple_of(sbi * SB, SB), SB)
        o = oacc_ref[rows, :]
        o = o * lax.rsqrt(jnp.mean(o * o, axis=-1, keepdims=True) + RMS_EPS) * nw
        o_ref[0, rows, :] = (o * _silu(z_ref[0, rows, :].astype(f32))).astype(bf16)

    def tail(t, carry):
        finish(t - 1)
        finish(nsb - t)
        sf, sbk = carry
        return sb_step(0, t, sf), sb_step(1, nsb - 1 - t, sbk)

    t0 = nsb // 2
    sf, sbk = states[0]
    carry = sb_step(0, t0, sf), sb_step(1, nsb - 1 - t0, sbk)
    lax.fori_loop(t0 + 1, nsb, tail, carry)
    finish(nsb - 1)
    finish(0)


def _gdn(proj3, ab3, conv_w, a_log, dt_bias, norm_w):
    B, S, _ = proj3.shape
    hb = GDN_HEAD_DIM
    nsb = S // SB

    def col(base):
        return pl.BlockSpec((1, S, hb), lambda b, h, base=base: (b, 0, base // hb + h))

    def cw(base):
        return pl.BlockSpec((GDN_CONV, hb), lambda b, h, base=base: (0, base // hb + h))

    smem = pl.BlockSpec(memory_space=pltpu.SMEM)
    return pl.pallas_call(
        functools.partial(_gdn_kernel, S=S, unroll=GDN_UNROLL),
        out_shape=jax.ShapeDtypeStruct((B, S, GDN_WIDTH), bf16),
        grid=(B, GDN_HEADS),
        in_specs=[smem, smem, col(COL_GQ), col(COL_GK), col(COL_GV), col(COL_GZ),
                  pl.BlockSpec((1, S, LANES), lambda b, h: (b, 0, 0)),
                  cw(0), cw(GDN_WIDTH), cw(2 * GDN_WIDTH),
                  pl.BlockSpec((1, hb), lambda b, h: (0, 0))],
        out_specs=pl.BlockSpec((1, S, hb), lambda b, h: (b, 0, h)),
        scratch_shapes=[pltpu.VMEM((S, hb), f32), pltpu.VMEM((S, hb), f32), pltpu.VMEM((S, hb), f32),
                        pltpu.VMEM((3, S + 2 * CONV_PAD, hb), f32),
                        pltpu.VMEM((2, nsb * 2 * SB, hb), bf16),
                        pltpu.VMEM((2, nsb * SB, hb), f32),
                        pltpu.VMEM((2, nsb * 8, hb), f32),
                        pltpu.VMEM((S, hb), f32)],
        compiler_params=_cparams(("parallel", "arbitrary")),
        name="gdn",
    )(a_log, dt_bias, proj3, proj3, proj3, proj3, ab3, conv_w, conv_w, conv_w, norm_w)


N_BIAS_TILES = 6
KV_TILE = 2 * LANES


def _bias_kernel(rb_ref, bucket_ref, lp_ref, bias_ref, lam_ref):
    bucket = bucket_ref[...]
    for hh in range(DIFF_HEADS):
        acc = jnp.zeros(bucket.shape, f32)
        for bk in range(NUM_BUCKETS):
            acc = jnp.where(bucket == bk, rb_ref[bk, hh], acc)
        bias_ref[hh] = acc
    lp = lp_ref[...]
    lam = (jnp.exp(jnp.sum(lp[0:1] * lp[1:2], axis=-1, keepdims=True))
           - jnp.exp(jnp.sum(lp[2:3] * lp[3:4], axis=-1, keepdims=True)) + LAMBDA_INIT)
    lam_ref[...] = jnp.broadcast_to(lam, lam_ref.shape)


def _t5_bucket(rel):
    nb = NUM_BUCKETS // 2
    max_exact = nb // 2
    ret = jnp.where(rel > 0, nb, 0)
    n = jnp.abs(rel)
    nf = jnp.maximum(n, 1).astype(jnp.float32)
    large = max_exact + (jnp.log(nf / max_exact) / math.log(MAX_DISTANCE / max_exact)
                         * (nb - max_exact)).astype(jnp.int32)
    large = jnp.minimum(large, nb - 1)
    return ret + jnp.where(n < max_exact, n, large)


def _bias_tiles(rel_bias, lam_params):
    delta = jnp.arange(N_BIAS_TILES, dtype=jnp.int32)[:, None, None] - 3
    rel = (LANES * delta + jnp.arange(KV_TILE, dtype=jnp.int32)[None, None, :]
           - jnp.arange(LANES, dtype=jnp.int32)[None, :, None])
    bucket = _t5_bucket(rel).astype(jnp.int32)
    return pl.pallas_call(
        _bias_kernel,
        out_shape=(jax.ShapeDtypeStruct((DIFF_HEADS, N_BIAS_TILES, LANES, KV_TILE), f32),
                   jax.ShapeDtypeStruct((8, LANES), f32)),
        in_specs=[pl.BlockSpec(memory_space=pltpu.SMEM),
                  pl.BlockSpec(memory_space=pltpu.VMEM),
                  pl.BlockSpec(memory_space=pltpu.VMEM)],
        out_specs=(pl.BlockSpec(memory_space=pltpu.VMEM), pl.BlockSpec(memory_space=pltpu.VMEM)),
        name="rel_bias",
    )(rel_bias, bucket, lam_params)


def _diff_kernel(q_ref, k_ref, v_ref, bias_ref, lam_ref, nw_ref, o_ref, s_ref, e_ref, *, S, QB):
    qi = pl.program_id(2)
    n_tiles = S // KV_TILE
    nsub = QB // LANES
    dh = DIFF_HEAD_DIM
    q = (q_ref[0].astype(f32) * (dh ** -0.5)).astype(bf16)

    def tile(jt):
        return slice(jt * KV_TILE, (jt + 1) * KV_TILE)

    def score_tile(m, jt, mx):
        s = lax.dot_general(q[:, m * dh:(m + 1) * dh], k_ref[0, tile(jt), m * dh:(m + 1) * dh], NT_DIMS,
                            preferred_element_type=f32)
        parts = []
        for r in range(nsub):
            idx = jnp.clip(2 * jt - (qi * nsub + r), -3, 2) + 3
            parts.append(s[r * LANES:(r + 1) * LANES, :] + bias_ref[0, idx])
        s = jnp.concatenate(parts, axis=0) if nsub > 1 else parts[0]
        s_ref[m, :, tile(jt)] = s
        return jnp.maximum(mx, jnp.maximum(s[:, :LANES], s[:, LANES:]))

    def exp_tile(m, jt, row_max, den):
        e = jnp.exp(s_ref[m, :, tile(jt)] - row_max)
        e_ref[m, :, tile(jt)] = e.astype(bf16)
        return den + (e[:, :LANES] + e[:, LANES:])

    def pv_tile(m, jt, acc):
        return acc + jnp.dot(e_ref[m, :, tile(jt)], v_ref[0, tile(jt), :], preferred_element_type=f32)

    neg = jnp.full((QB, LANES), -jnp.inf, f32)
    zero = jnp.zeros((QB, LANES), f32)
    mx0 = neg
    for jt in range(n_tiles):
        mx0 = score_tile(0, jt, mx0)
    max0 = jnp.max(mx0, axis=-1, keepdims=True)
    mx1, den0 = neg, zero
    for jt in range(n_tiles):
        mx1 = score_tile(1, jt, mx1)
        den0 = exp_tile(0, jt, max0, den0)
    max1 = jnp.max(mx1, axis=-1, keepdims=True)
    pv0, den1 = jnp.zeros((QB, 2 * dh), f32), zero
    for jt in range(n_tiles):
        pv0 = pv_tile(0, jt, pv0)
        den1 = exp_tile(1, jt, max1, den1)
    pv1 = jnp.zeros((QB, 2 * dh), f32)
    for jt in range(n_tiles):
        pv1 = pv_tile(1, jt, pv1)
    outs = [pv0 / jnp.sum(den0, axis=-1, keepdims=True), pv1 / jnp.sum(den1, axis=-1, keepdims=True)]
    lam = lam_ref[0:1, 0:1]
    o = outs[0] - lam * outs[1]
    o = o * lax.rsqrt(jnp.mean(o * o, axis=-1, keepdims=True) + RMS_EPS) * nw_ref[...]
    o_ref[0] = (o * (1.0 - LAMBDA_INIT)).astype(bf16)


def _diff_attention(proj3, bias_tiles, lam, norm_w, *, qb):
    B, S, _ = proj3.shape
    w = 2 * DIFF_HEAD_DIM
    return pl.pallas_call(
        functools.partial(_diff_kernel, S=S, QB=qb),
        out_shape=jax.ShapeDtypeStruct((B, S, DIFF_HEADS * w), bf16),
        grid=(B, DIFF_HEADS, S // qb),
        in_specs=[pl.BlockSpec((1, qb, w), lambda b, h, i: (b, i, COL_DQ // w + h)),
                  pl.BlockSpec((1, S, w), lambda b, h, i: (b, 0, COL_DK // w + h)),
                  pl.BlockSpec((1, S, w), lambda b, h, i: (b, 0, COL_DV // w + h)),
                  pl.BlockSpec((1, N_BIAS_TILES, LANES, KV_TILE), lambda b, h, i: (h, 0, 0, 0)),
                  pl.BlockSpec((8, LANES), lambda b, h, i: (0, 0)),
                  pl.BlockSpec((1, w), lambda b, h, i: (0, 0))],
        out_specs=pl.BlockSpec((1, qb, w), lambda b, h, i: (b, i, h)),
        scratch_shapes=[pltpu.VMEM((2, qb, S), f32), pltpu.VMEM((2, qb, S), bf16)],
        compiler_params=_cparams(("parallel", "parallel", "arbitrary")),
        name="diff_attn",
    )(proj3, proj3, proj3, bias_tiles, lam, norm_w)


def _cross_kernel(q_ref, kv_ref, o_ref):
    w = CROSS_HEAD_DIM
    heads = range(CROSS_HEADS)
    s = [lax.dot_general(q_ref[0, :, h * w:(h + 1) * w], kv_ref[0, :, h * w:(h + 1) * w], NT_DIMS,
                         preferred_element_type=f32) * (w ** -0.5) for h in heads]
    e = [jnp.exp(s[h] - jnp.max(s[h], axis=-1, keepdims=True)) for h in heads]
    p = [(e[h] / jnp.sum(e[h], axis=-1, keepdims=True)).astype(bf16) for h in heads]
    for h in heads:
        o_ref[0, :, h * w:(h + 1) * w] = jnp.dot(
            p[h], kv_ref[0, :, (CROSS_HEADS + h) * w:(CROSS_HEADS + h + 1) * w],
            preferred_element_type=f32).astype(bf16)


def _cross_attention(proj3, kv3, *, qb):
    B, S, _ = proj3.shape
    M = kv3.shape[1]
    cw = CROSS_HEADS * CROSS_HEAD_DIM
    return pl.pallas_call(
        _cross_kernel,
        out_shape=jax.ShapeDtypeStruct((B, S, cw), bf16),
        grid=(B, S // qb),
        in_specs=[pl.BlockSpec((1, qb, cw), lambda b, i: (b, i, COL_CQ // cw)),
                  pl.BlockSpec((1, M, 2 * cw), lambda b, i: (b, 0, 0))],
        out_specs=pl.BlockSpec((1, qb, cw), lambda b, i: (b, i, 0)),
        compiler_params=_cparams(("parallel", "arbitrary")),
        name="cross_attn",
    )(proj3, kv3)


def _merge_kernel(yg_ref, yd_ref, yc_ref, g0_ref, g1_ref, g2_ref, wg_ref, wd_ref, wc_ref, o_ref):
    m = g0_ref[...].astype(f32) * jnp.dot(yg_ref[...], wg_ref[...], preferred_element_type=f32)
    m = m + g1_ref[...].astype(f32) * jnp.dot(yd_ref[...], wd_ref[...], preferred_element_type=f32)
    m = m + g2_ref[...].astype(f32) * jnp.dot(yc_ref[...], wc_ref[...], preferred_element_type=f32)
    o_ref[...] = m.astype(bf16)


def _merge(yg, yd, yc, proj2, wg, wd, wc, *, tm):
    T = yg.shape[0]
    gate_blk = COL_GATE // D_MODEL

    def yspec():
        return pl.BlockSpec((tm, GDN_WIDTH), lambda i: (i, 0))

    def gspec(n):
        return pl.BlockSpec((tm, D_MODEL), lambda i, n=n: (i, gate_blk + n))

    def wspec():
        return pl.BlockSpec((GDN_WIDTH, D_MODEL), lambda i: (0, 0), pipeline_mode=pl.Buffered(1))

    return pl.pallas_call(
        _merge_kernel,
        out_shape=jax.ShapeDtypeStruct((T, D_MODEL), bf16),
        grid=(T // tm,),
        in_specs=[yspec(), yspec(), yspec(), gspec(0), gspec(1), gspec(2), wspec(), wspec(), wspec()],
        out_specs=pl.BlockSpec((tm, D_MODEL), lambda i: (i, 0)),
        compiler_params=_cparams(("parallel",)),
        name="merge",
    )(yg, yd, yc, proj2, proj2, proj2, wg, wd, wc)


def _outproj_kernel(m_ref, x_ref, w_ref, g_ref, b_ref, o_ref):
    y = DEEPNORM_ALPHA * x_ref[...] + jnp.dot(m_ref[...], w_ref[...], preferred_element_type=f32)
    o_ref[...] = _layer_norm(y, g_ref[...], b_ref[...])


def _outproj_ln(merged, x2, w_out, g, b, *, tm):
    T = x2.shape[0]
    return pl.pallas_call(
        _outproj_kernel,
        out_shape=jax.ShapeDtypeStruct((T, D_MODEL), f32),
        grid=(T // tm,),
        in_specs=[pl.BlockSpec((tm, D_MODEL), lambda i: (i, 0)),
                  pl.BlockSpec((tm, D_MODEL), lambda i: (i, 0)),
                  pl.BlockSpec((D_MODEL, D_MODEL), lambda i: (0, 0), pipeline_mode=pl.Buffered(1)),
                  pl.BlockSpec((1, D_MODEL), lambda i: (0, 0)),
                  pl.BlockSpec((1, D_MODEL), lambda i: (0, 0))],
        out_specs=pl.BlockSpec((tm, D_MODEL), lambda i: (i, 0)),
        compiler_params=_cparams(("parallel",)),
        name="outproj_ln",
    )(merged, x2, w_out, g, b)


HALO = 8
FFN_TN = 512
FFN_SUB = 128
FFN_ROWS = 128


def _ffn_kernel(xm_ref, xp_ref, xn_ref, wg_ref, wv_ref, cg_ref, cv_ref, wdn_ref, g_ref, b_ref, o_ref,
                xh_ref, up_ref, h_ref, *, tm, tn, tiles_per_seq):
    i = pl.program_id(0)
    n = pl.program_id(1)

    @pl.when(n == 0)
    def _():
        pos = i % tiles_per_seq
        keep_prev = (pos != 0).astype(f32)
        keep_next = (pos != tiles_per_seq - 1).astype(f32)
        xh_ref[0:HALO, :] = (xp_ref[...] * keep_prev).astype(bf16)
        xh_ref[HALO:HALO + tm, :] = xm_ref[...].astype(bf16)
        xh_ref[HALO + tm:2 * HALO + tm, :] = (xn_ref[...] * keep_next).astype(bf16)
        o_ref[...] = DEEPNORM_ALPHA * xm_ref[...]

    taps = (cg_ref[...], cv_ref[...])
    planes = FFN_SUB // LANES

    def up_dot(c):
        cols = slice(c * FFN_SUB, (c + 1) * FFN_SUB)
        w = jnp.concatenate([wg_ref[:, cols], wv_ref[:, cols]], axis=1)
        up = jnp.dot(xh_ref[...], w, preferred_element_type=f32)
        for q in range(2 * planes):
            up_ref[c % 2, q] = up[:, q * LANES:(q + 1) * LANES]

    def act(c):
        for p in range(planes):
            c0 = c * FFN_SUB + p * LANES
            for r in range(tm // FFN_ROWS):
                r0 = r * FFN_ROWS
                conv = [sum(up_ref[c % 2, gv * planes + p, r0 + HALO - 1 + j:r0 + HALO - 1 + j + FFN_ROWS, :]
                            * taps[gv][j:j + 1, c0:c0 + LANES] for j in range(3)) for gv in range(2)]
                h_ref[r0:r0 + FFN_ROWS, c0:c0 + LANES] = (_silu(conv[0]) * conv[1]).astype(bf16)

    nsub = tn // FFN_SUB
    up_dot(0)
    for c in range(nsub):
        if c + 1 < nsub:
            up_dot(c + 1)
        act(c)
    o_ref[...] += jnp.dot(h_ref[...], wdn_ref[...], preferred_element_type=f32)

    @pl.when(n == pl.num_programs(1) - 1)
    def _():
        o_ref[...] = _layer_norm(o_ref[...], g_ref[...], b_ref[...])


def _ffn_ln(x1, w_up_p, conv_p, w_down_p, g, b, *, S, tm, tn):
    T = x1.shape[0]
    nf = D_FF_PAD // tn
    hpt = tm // HALO
    last_halo = T // HALO - 1
    return pl.pallas_call(
        functools.partial(_ffn_kernel, tm=tm, tn=tn, tiles_per_seq=S // tm),
        out_shape=jax.ShapeDtypeStruct((T, D_MODEL), f32),
        grid=(T // tm, nf),
        in_specs=[pl.BlockSpec((tm, D_MODEL), lambda i, n: (i, 0)),
                  pl.BlockSpec((HALO, D_MODEL), lambda i, n: (jnp.maximum(i * hpt - 1, 0), 0)),
                  pl.BlockSpec((HALO, D_MODEL), lambda i, n: (jnp.minimum((i + 1) * hpt, last_halo), 0)),
                  pl.BlockSpec((D_MODEL, tn), lambda i, n: (0, n)),
                  pl.BlockSpec((D_MODEL, tn), lambda i, n: (0, nf + n)),
                  pl.BlockSpec((3, tn), lambda i, n: (0, n)),
                  pl.BlockSpec((3, tn), lambda i, n: (0, nf + n)),
                  pl.BlockSpec((tn, D_MODEL), lambda i, n: (n, 0)),
                  pl.BlockSpec((1, D_MODEL), lambda i, n: (0, 0)),
                  pl.BlockSpec((1, D_MODEL), lambda i, n: (0, 0))],
        out_specs=pl.BlockSpec((tm, D_MODEL), lambda i, n: (i, 0)),
        scratch_shapes=[pltpu.VMEM((tm + 2 * HALO, D_MODEL), bf16),
                        pltpu.VMEM((2, 2 * FFN_SUB // LANES, tm + 2 * HALO, LANES), f32),
                        pltpu.VMEM((tm, tn), bf16)],
        compiler_params=_cparams(("parallel", "arbitrary")),
        name="ffn_ln",
    )(x1, x1, x1, w_up_p, w_up_p, conv_p, conv_p, w_down_p, g, b)


def _pick(n, prefs):
    for p in prefs:
        if n % p == 0:
            return p
    raise ValueError(f"no tile in {prefs} divides {n}")


def _trunk(x, mem, wts):
    B, S, _ = x.shape
    T = B * S
    assert S % CONV_ROWS == 0 and S % KV_TILE == 0 and (S // SB) % GDN_UNROLL == 0
    x2 = x.reshape(T, D_MODEL)

    proj2, ab2 = _inproj(x2, wts["w_all"], wts["b_all"], wts["w_ab"], tm=_pick(T, (1024, 512, 256)), tn=1024)
    proj3 = proj2.reshape(B, S, PROJ_COLS)
    ab3 = ab2.reshape(B, S, LANES)

    y_gdn = _gdn(proj3, ab3, wts["gdn_conv"], wts["a_log"], wts["dt_bias"], wts["gdn_norm_w"])
    y_diff = _diff_attention(proj3, wts["bias_tiles"], wts["lam"], wts["diff_norm_w"], qb=_pick(S, (512, 256)))
    M = mem.shape[1]
    kv = _matmul(mem.reshape(B * M, D_MODEL), wts["w_mem_kv"], tm=_pick(B * M, (512, 256)))
    y_cross = _cross_attention(proj3, kv.reshape(B, M, 2 * CROSS_HEADS * CROSS_HEAD_DIM), qb=_pick(S, (1024, 512, 256)))

    tm = _pick(S, (512, 256))
    merged = _merge(y_gdn.reshape(T, -1), y_diff.reshape(T, -1), y_cross.reshape(T, -1), proj2,
                    wts["w_bg"], wts["w_bd"], wts["w_bc"], tm=tm)
    x1 = _outproj_ln(merged, x2, wts["w_out"], wts["ln1_g"], wts["ln1_b"], tm=tm)
    y = _ffn_ln(x1, wts["w_up_p"], wts["ffn_conv_p"], wts["w_down_p"], wts["ln2_g"], wts["ln2_b"],
                S=S, tm=tm, tn=FFN_TN)
    return y.reshape(B, S, D_MODEL)


def _prep_weights(rel_bias, w_in, gdn_conv, gdn_a_log, gdn_dt_bias, gdn_norm_w, diff_lambda, diff_norm_w,
                  w_mem_kv, w_gate, b_gate, w_branch_gdn, w_branch_diff, w_branch_cross, w_out,
                  ln1_g, ln1_b, w_up, ffn_conv, w_down, ln2_g, ln2_b):
    l = 0
    wi = w_in[l]
    a0 = 4 * GDN_WIDTH
    a1 = a0 + 4 * GDN_HEADS
    w_lin = jnp.concatenate([wi[:, :a0], wi[:, a1:]], axis=1)
    w_all = jnp.concatenate([w_lin, w_gate[l]], axis=1).astype(bf16)
    b_all = jnp.concatenate([jnp.zeros((PROJ_LIN,), f32), b_gate[l].astype(f32)])[None, :]
    w_ab = jnp.pad(wi[:, a0:a1], ((0, 0), (0, LANES - 4 * GDN_HEADS))).astype(bf16)

    padc = D_FF_PAD - D_FF

    def pad_halves(a):
        zeros = jnp.zeros((a.shape[0], padc), a.dtype)
        return jnp.concatenate([a[:, :D_FF], zeros, a[:, D_FF:], zeros], axis=1)

    bias_tiles, lam = _bias_tiles(rel_bias.astype(f32), diff_lambda[l].astype(f32))
    return dict(
        w_all=w_all, b_all=b_all, w_ab=w_ab,
        gdn_conv=gdn_conv[l].astype(f32), a_log=gdn_a_log[l].astype(f32), dt_bias=gdn_dt_bias[l].astype(f32),
        gdn_norm_w=gdn_norm_w[l].astype(f32)[None, :],
        bias_tiles=bias_tiles, lam=lam, diff_norm_w=diff_norm_w[l].astype(f32)[None, :],
        w_mem_kv=w_mem_kv[l].astype(bf16),
        w_bg=w_branch_gdn[l].astype(bf16), w_bd=w_branch_diff[l].astype(bf16), w_bc=w_branch_cross[l].astype(bf16),
        w_out=w_out[l].astype(bf16),
        ln1_g=ln1_g[l].astype(f32)[None, :], ln1_b=ln1_b[l].astype(f32)[None, :],
        w_up_p=pad_halves(w_up[l].astype(bf16)), ffn_conv_p=pad_halves(ffn_conv[l].astype(f32)),
        w_down_p=jnp.pad(w_down[l], ((0, padc), (0, 0))).astype(bf16),
        ln2_g=ln2_g[l].astype(f32)[None, :], ln2_b=ln2_b[l].astype(f32)[None, :],
    )


def kernel(x_prompt, x_sample, mem_prompt, mem_sample, rel_bias, w_in, gdn_conv, gdn_a_log, gdn_dt_bias, gdn_norm_w, diff_lambda, diff_norm_w, w_mem_kv, w_gate, b_gate, w_branch_gdn, w_branch_diff, w_branch_cross, w_out, ln1_g, ln1_b, w_up, ffn_conv, w_down, ln2_g, ln2_b):
    wts = _prep_weights(rel_bias, w_in, gdn_conv, gdn_a_log, gdn_dt_bias, gdn_norm_w, diff_lambda, diff_norm_w,
                        w_mem_kv, w_gate, b_gate, w_branch_gdn, w_branch_diff, w_branch_cross, w_out,
                        ln1_g, ln1_b, w_up, ffn_conv, w_down, ln2_g, ln2_b)
    return (_trunk(x_prompt, mem_prompt, wts), _trunk(x_sample, mem_sample, wts))
```

```python
import functools
import math

import jax
import jax.numpy as jnp
import numpy as np
from jax import lax
from jax.experimental import pallas as pl
from jax.experimental.pallas import tpu as pltpu

f32 = jnp.float32
bf16 = jnp.bfloat16

D_MODEL = 2048
GDN_HEADS = 8
GDN_HEAD_DIM = 128
GDN_WIDTH = 1024
GDN_CONV = 5
GDN_CHUNK = 64
DIFF_HEADS = 4
DIFF_HEAD_DIM = 128
CROSS_HEADS = 4
CROSS_HEAD_DIM = 256
N_BRANCH = 3
D_FF = 5504
NUM_BUCKETS = 32
MAX_DISTANCE = 128
LN_EPS = 1e-5
RMS_EPS = 1e-6
L2_EPS = 1e-6
DEPTH = 1
DEEPNORM_ALPHA = (2 * DEPTH) ** 0.25
LAMBDA_INIT = 0.8 - 0.6 * math.exp(-0.3 * 0)

LANES = 128
MXU_WIDTH = 256
VMEM_LIMIT_BYTES = 56 * 1024 * 1024

PROJ_LIN = 8192
COL_GQ, COL_GK, COL_GV, COL_GZ = 0, 1024, 2048, 3072
COL_DQ, COL_DK, COL_DV, COL_CQ = 4096, 5120, 6144, 7168
COL_GATE = PROJ_LIN
PROJ_COLS = PROJ_LIN + N_BRANCH * D_MODEL
D_FF_PAD = 5632

NT_DIMS = (((1,), (1,)), ((), ()))


def _cparams(sem, vmem=VMEM_LIMIT_BYTES):
    return pltpu.CompilerParams(dimension_semantics=sem, vmem_limit_bytes=vmem)


def _sigmoid(x):
    return 1.0 / (1.0 + jnp.exp(-x))


def _silu(x):
    return x * _sigmoid(x)


def _layer_norm(y, g, b):
    mu = jnp.mean(y, axis=-1, keepdims=True)
    yc = y - mu
    var = jnp.mean(yc * yc, axis=-1, keepdims=True)
    return yc * lax.rsqrt(var + LN_EPS) * g + b


def _bdot(a, b):
    return jnp.dot(a.astype(bf16), b.astype(bf16), preferred_element_type=f32)


def _inproj_kernel(x_ref, w_ref, b_ref, wab_ref, o_ref, ab_ref, xb_ref, *, n_lin_tiles):
    j = pl.program_id(1)

    @pl.when(j == 0)
    def _():
        xb = x_ref[...].astype(bf16)
        xb_ref[...] = xb
        ab_ref[...] = jnp.dot(xb, wab_ref[...], preferred_element_type=f32)

    acc = jnp.dot(xb_ref[...], w_ref[...], preferred_element_type=f32)
    o_ref[...] = jnp.where(j >= n_lin_tiles, _sigmoid(acc + b_ref[...]), acc).astype(bf16)


def _inproj(x2, w_all, b_all, w_ab, *, tm, tn):
    T = x2.shape[0]
    n_lin_tiles = PROJ_LIN // tn
    return pl.pallas_call(
        functools.partial(_inproj_kernel, n_lin_tiles=n_lin_tiles),
        out_shape=(jax.ShapeDtypeStruct((T, PROJ_COLS), bf16),
                   jax.ShapeDtypeStruct((T, LANES), f32)),
        grid=(T // tm, PROJ_COLS // tn),
        in_specs=[pl.BlockSpec((tm, D_MODEL), lambda i, j: (i, 0)),
                  pl.BlockSpec((D_MODEL, tn), lambda i, j: (0, j)),
                  pl.BlockSpec((1, tn), lambda i, j: (0, j)),
                  pl.BlockSpec((D_MODEL, LANES), lambda i, j: (0, 0))],
        out_specs=(pl.BlockSpec((tm, tn), lambda i, j: (i, j)),
                   pl.BlockSpec((tm, LANES), lambda i, j: (i, 0))),
        scratch_shapes=[pltpu.VMEM((tm, D_MODEL), bf16)],
        compiler_params=_cparams(("parallel", "arbitrary")),
        name="inproj",
    )(x2, w_all, b_all, w_ab)


def _mm_kernel(x_ref, w_ref, o_ref):
    o_ref[...] = jnp.dot(x_ref[...].astype(bf16), w_ref[...], preferred_element_type=f32).astype(o_ref.dtype)


def _matmul(x2, w, *, tm, out_dtype=bf16):
    T, K = x2.shape
    N = w.shape[1]
    return pl.pallas_call(
        _mm_kernel,
        out_shape=jax.ShapeDtypeStruct((T, N), out_dtype),
        grid=(T // tm,),
        in_specs=[pl.BlockSpec((tm, K), lambda i: (i, 0)),
                  pl.BlockSpec((K, N), lambda i: (0, 0), pipeline_mode=pl.Buffered(1))],
        out_specs=pl.BlockSpec((tm, N), lambda i: (i, 0)),
        compiler_params=_cparams(("parallel",)),
        name="mem_kv",
    )(x2, w)


SB = 2 * GDN_CHUNK
CONV_PAD = 8
GDN_UNROLL = 8
INV_BASE = 8


def _split_bf16(x):
    hi = x.astype(bf16)
    lo = (x - hi.astype(f32)).astype(bf16)
    return hi, lo


def _gdn_kernel(alog_ref, dtb_ref, q_ref, k_ref, v_ref, z_ref, ab_ref, cq_ref, ck_ref, cv_ref,
                nw_ref, o_ref, xp_ref, lhs_ref, n_ref, egl_ref, oacc_ref, *, S, unroll):
    h = pl.program_id(1)
    nsb = S // SB
    hd = GDN_HEAD_DIM

    zpad = jnp.zeros((CONV_PAD, hd), f32)
    for si, src_ref in enumerate((q_ref, k_ref, v_ref)):
        xp_ref[si, 0:CONV_PAD, :] = zpad
        xp_ref[si, S + CONV_PAD:S + 2 * CONV_PAD, :] = zpad
        xp_ref[si, CONV_PAD:S + CONV_PAD, :] = src_ref[0].astype(f32)
    taps = (cq_ref[...], ck_ref[...], cv_ref[...])

    def conv_silu(si, sb):
        r0 = pl.multiple_of(sb * SB, SB)
        acc = jnp.zeros((SB, hd), f32)
        for j in range(GDN_CONV):
            acc = acc + xp_ref[si, pl.ds(r0 + CONV_PAD - GDN_CONV // 2 + j, SB), :] * taps[si][j:j + 1, :]
        y = _silu(acc)
        if si < 2:
            y = y * lax.rsqrt(jnp.sum(y * y, axis=-1, keepdims=True) + L2_EPS)
        if si == 0:
            y = y * (hd ** -0.5)
        return y

    ri = lax.broadcasted_iota(jnp.int32, (SB, SB), 0)
    ci = lax.broadcasted_iota(jnp.int32, (SB, SB), 1)

    def same(blk):
        return (ri // blk) == (ci // blk)

    same_chunk = same(GDN_CHUNK)
    incl = (same_chunk & (ri >= ci), same_chunk & (ri <= ci))
    strict = (same_chunk & (ri > ci), same_chunk & (ri < ci))
    same_base = same(INV_BASE)
    merge_masks = []
    blk = INV_BASE
    while blk < GDN_CHUNK:
        merge_masks.append(same(2 * blk) & jnp.logical_not(same(blk)))
        blk *= 2
    eye = (ri == ci).astype(f32)
    tri2 = [jnp.concatenate([incl[d].astype(bf16)] * 2, axis=1) for d in range(2)]
    r4 = lax.broadcasted_iota(jnp.int32, (2 * SB, 4 * hd), 0) % SB
    c4 = lax.broadcasted_iota(jnp.int32, (2 * SB, 4 * hd), 1) // hd
    sel = (r4 == (c4 * GDN_HEADS + h)).astype(bf16)
    rh = lax.broadcasted_iota(jnp.int32, (SB, 4 * hd), 0) < GDN_CHUNK
    ch = lax.broadcasted_iota(jnp.int32, (SB, 4 * hd), 1) < 2 * hd
    half_mask = rh == ch

    ends = unroll // 2
    n_groups = nsb // unroll

    def chunk_local(i, rec=()):
        rec = list(rec)

        def run_rec():
            if rec:
                rec.pop(0)()

        chains = []
        for u in range(unroll):
            sb = i * ends + u if u < ends else nsb - (i + 1) * ends + (u - ends)
            rows = pl.ds(pl.multiple_of(sb * SB, SB), SB)
            qn, kn, vn = (conv_silu(si, sb) for si in range(3))
            knb = kn.astype(bf16)
            kq = lax.dot_general(jnp.concatenate([knb, qn.astype(bf16)], axis=0), knb, NT_DIMS,
                                 preferred_element_type=f32)
            ab_hi, ab_lo = _split_bf16(ab_ref[0, rows, :])
            absel = jnp.dot(jnp.concatenate([ab_hi, ab_lo], axis=1), sel, preferred_element_type=f32)
            for d in range(2):
                chains.append(dict(sb=sb, rows=rows, d=d, kn=kn, qn=qn, vn=vn, kk=kq[:SB], qk=kq[SB:],
                                   a=absel[:, d * hd:(d + 1) * hd], b=absel[:, (2 + d) * hd:(3 + d) * hd]))
        for t in chains:
            d = t["d"]
            xs = t["a"] + dtb_ref[d, h]
            softplus = jnp.maximum(xs, 0.0) + jnp.log1p(jnp.exp(-jnp.abs(xs)))
            g = -jnp.exp(jnp.full((1, hd), alog_ref[d, h], f32)) * softplus
            t["beta"] = _sigmoid(t["b"])
            g_hi, g_lo = _split_bf16(g)
            t["gc"] = jnp.dot(tri2[d], jnp.concatenate([g_hi, g_lo], axis=0),
                              preferred_element_type=f32)
        run_rec()
        for t in chains:
            d, gc, beta = t["d"], t["gc"], t["beta"]
            diff = gc - gc.T
            t["decay"] = jnp.where(incl[d], jnp.exp(jnp.where(incl[d], diff, 0.0)), 0.0)
            t["l"] = jnp.where(strict[d], beta * t["kk"] * t["decay"], 0.0)
            t["egc"] = jnp.exp(gc)
            t["rhs"] = jnp.concatenate([beta * t["vn"], (beta * t["egc"]) * t["kn"]], axis=1)
            t["ld"] = jnp.where(same_base, t["l"], 0.0)
            t["ld2"] = _bdot(t["ld"], t["ld"])
        run_rec()
        for t in chains:
            iml = eye - t["ld"]
            t["x"] = iml + _bdot(iml, t["ld2"])
            t["ld4"] = _bdot(t["ld2"], t["ld2"])
        run_rec()
        for t in chains:
            t["t"] = t["x"] + _bdot(t["x"], t["ld4"])
        run_rec()
        for mask in merge_masks:
            for t in chains:
                t["w"] = _bdot(t["t"], jnp.where(mask, t["l"], 0.0))
            run_rec()
            for t in chains:
                t["t"] = t["t"] - _bdot(t["w"], t["t"])
            run_rec()
        for t in chains:
            t["sol"] = _bdot(t["t"], t["rhs"])
        while rec:
            run_rec()
        for t in chains:
            d, gc, sol = t["d"], t["gc"], t["sol"]
            w = jnp.concatenate([sol[:, hd:], sol[:, :hd]], axis=1).astype(bf16)
            qkd = jnp.where(incl[d], t["qk"] * t["decay"], 0.0).astype(bf16)
            t["qw"] = jnp.dot(qkd, w, preferred_element_type=f32)
            last = (GDN_CHUNK - 1) if d == 0 else 0
            gl = jnp.concatenate(
                [jnp.broadcast_to(gc[last:last + 1, :], (GDN_CHUNK, hd)),
                 jnp.broadcast_to(gc[GDN_CHUNK + last:GDN_CHUNK + last + 1, :], (GDN_CHUNK, hd))], axis=0)
            kdt = (t["kn"] * jnp.exp(gl - gc)).T.astype(bf16)
            wpair = jnp.where(half_mask, jnp.concatenate([w, w], axis=1), jnp.zeros((SB, 4 * hd), bf16))
            t["kw"] = jnp.dot(kdt, wpair, preferred_element_type=f32)
            t["egl"] = jnp.exp(gl)
        for t in chains:
            d = t["d"]
            f_, s_ = (0, 1) if d == 0 else (1, 0)
            kw = t["kw"]
            t["qp"] = t["qn"] * t["egc"] - t["qw"][:, :hd]
            a = [-kw[:, 2 * hf * hd:(2 * hf + 1) * hd] for hf in range(2)]
            nn = [kw[:, (2 * hf + 1) * hd:(2 * hf + 2) * hd] for hf in range(2)]
            qs = t["qp"][s_ * GDN_CHUNK:(s_ + 1) * GDN_CHUNK, :]
            t["am"], t["nm"] = a, nn
            t["xc"] = _bdot(jnp.concatenate([a[s_], qs], axis=0), jnp.concatenate([a[f_], nn[f_]], axis=1))
        for t in chains:
            d, sb = t["d"], t["sb"]
            f_, s_ = (0, 1) if d == 0 else (1, 0)
            a, nn, x, qp = t["am"], t["nm"], t["xc"], t["qp"]
            dec = [t["egl"][hf * GDN_CHUNK:hf * GDN_CHUNK + 1, :] for hf in range(2)]
            a2 = dec[s_] * a[f_] + dec[f_] * a[s_] + x[:hd, :hd]
            n2 = dec[s_] * nn[f_] + x[:hd, hd:] + nn[s_]
            q_f = qp[f_ * GDN_CHUNK:(f_ + 1) * GDN_CHUNK, :]
            q_s = dec[f_] * qp[s_ * GDN_CHUNK:(s_ + 1) * GDN_CHUNK, :] + x[hd:, :hd]
            q2 = jnp.concatenate([q_f, q_s] if d == 0 else [q_s, q_f], axis=0)
            zero = jnp.zeros((GDN_CHUNK, hd), f32)
            t["o"] = t["qw"][:, hd:] + jnp.concatenate([zero, x[hd:, hd:]] if d == 0 else [x[hd:, hd:], zero], axis=0)
            l0 = pl.multiple_of(sb * 2 * SB, 2 * SB)
            lhs_ref[d, pl.ds(l0, SB), :] = a2.astype(bf16)
            lhs_ref[d, pl.ds(l0 + SB, SB), :] = q2.astype(bf16)
            n_ref[d, pl.ds(pl.multiple_of(sb * SB, SB), SB), :] = n2
            egl_ref[d, pl.ds(pl.multiple_of(sb * 8, 8), 8), :] = jnp.broadcast_to(dec[0] * dec[1], (8, hd))
        for tf, tb in zip(chains[0::2], chains[1::2]):
            oacc_ref[tf["rows"], :] = tf["o"] + tb["o"]

    def sb_step(d, sbi, state):
        lhs = lhs_ref[d, pl.ds(pl.multiple_of(sbi * 2 * SB, 2 * SB), 2 * SB), :]
        pop = jnp.dot(lhs, state.astype(bf16), preferred_element_type=f32)
        rows = pl.ds(pl.multiple_of(sbi * SB, SB), SB)
        oacc_ref[rows, :] += pop[SB:, :]
        dec = egl_ref[d, pl.ds(pl.multiple_of(sbi * 8, 8), 8), :][0:1, :]
        return dec * state + pop[:SB, :] + n_ref[d, rows, :]

    def rec_steps(g, states):
        def step(j):
            def run():
                sf, sbk = states[0]
                t = g * ends + j
                states[0] = (sb_step(0, t, sf), sb_step(1, nsb - 1 - t, sbk))
            return run
        return [step(j) for j in range(ends)]

    z0 = jnp.zeros((hd, hd), f32)
    chunk_local(0)

    def piped(i, carry):
        states = [carry]
        chunk_local(i, rec_steps(i - 1, states))
        return states[0]

    carry = lax.fori_loop(1, n_groups, piped, (z0, z0))
    states = [carry]
    for run in rec_steps(n_groups - 1, states):
        run()

    nw = nw_ref[...]

    def finish(sbi):
        rows = pl.ds(pl.multiple_of(sbi * SB, SB), SB)
        o = oacc_ref[rows, :]
        o = o * lax.rsqrt(jnp.mean(o * o, axis=-1, keepdims=True) + RMS_EPS) * nw
        o_ref[0, rows, :] = (o * _silu(z_ref[0, rows, :].astype(f32))).astype(bf16)

    def tail(t, carry):
        finish(t - 1)
        finish(nsb - t)
        sf, sbk = carry
        return sb_step(0, t, sf), sb_step(1, nsb - 1 - t, sbk)

    t0 = nsb // 2
    sf, sbk = states[0]
    carry = sb_step(0, t0, sf), sb_step(1, nsb - 1 - t0, sbk)
    lax.fori_loop(t0 + 1, nsb, tail, carry)
    finish(nsb - 1)
    finish(0)


def _gdn(proj3, ab3, conv_w, a_log, dt_bias, norm_w):
    B, S, _ = proj3.shape
    hb = GDN_HEAD_DIM
    nsb = S // SB

    def col(base):
        return pl.BlockSpec((1, S, hb), lambda b, h, base=base: (b, 0, base // hb + h))

    def cw(base):
        return pl.BlockSpec((GDN_CONV, hb), lambda b, h, base=base: (0, base // hb + h))

    smem = pl.BlockSpec(memory_space=pltpu.SMEM)
    return pl.pallas_call(
        functools.partial(_gdn_kernel, S=S, unroll=GDN_UNROLL),
        out_shape=jax.ShapeDtypeStruct((B, S, GDN_WIDTH), bf16),
        grid=(B, GDN_HEADS),
        in_specs=[smem, smem, col(COL_GQ), col(COL_GK), col(COL_GV), col(COL_GZ),
                  pl.BlockSpec((1, S, LANES), lambda b, h: (b, 0, 0)),
                  cw(0), cw(GDN_WIDTH), cw(2 * GDN_WIDTH),
                  pl.BlockSpec((1, hb), lambda b, h: (0, 0))],
        out_specs=pl.BlockSpec((1, S, hb), lambda b, h: (b, 0, h)),
        scratch_shapes=[pltpu.VMEM((3, S + 2 * CONV_PAD, hb), f32),
                        pltpu.VMEM((2, nsb * 2 * SB, hb), bf16),
                        pltpu.VMEM((2, nsb * SB, hb), f32),
                        pltpu.VMEM((2, nsb * 8, hb), f32),
                        pltpu.VMEM((S, hb), f32)],
        compiler_params=_cparams(("parallel", "arbitrary")),
        name="gdn",
    )(a_log, dt_bias, proj3, proj3, proj3, proj3, ab3, conv_w, conv_w, conv_w, norm_w)


N_BIAS_TILES = 6
KV_TILE = 2 * LANES


def _bias_kernel(rb_ref, bucket_ref, lp_ref, bias_ref, lam_ref):
    bucket = bucket_ref[...]
    for hh in range(DIFF_HEADS):
        acc = jnp.zeros(bucket.shape, f32)
        for bk in range(NUM_BUCKETS):
            acc = jnp.where(bucket == bk, rb_ref[bk, hh], acc)
        bias_ref[hh] = acc
    lp = lp_ref[...]
    lam = (jnp.exp(jnp.sum(lp[0:1] * lp[1:2], axis=-1, keepdims=True))
           - jnp.exp(jnp.sum(lp[2:3] * lp[3:4], axis=-1, keepdims=True)) + LAMBDA_INIT)
    lam_ref[...] = jnp.broadcast_to(lam, lam_ref.shape)


def _t5_bucket(rel):
    nb = NUM_BUCKETS // 2
    max_exact = nb // 2
    ret = jnp.where(rel > 0, nb, 0)
    n = jnp.abs(rel)
    nf = jnp.maximum(n, 1).astype(jnp.float32)
    large = max_exact + (jnp.log(nf / max_exact) / math.log(MAX_DISTANCE / max_exact)
                         * (nb - max_exact)).astype(jnp.int32)
    large = jnp.minimum(large, nb - 1)
    return ret + jnp.where(n < max_exact, n, large)


def _bias_tiles(rel_bias, lam_params):
    delta = jnp.arange(N_BIAS_TILES, dtype=jnp.int32)[:, None, None] - 3
    rel = (LANES * delta + jnp.arange(KV_TILE, dtype=jnp.int32)[None, None, :]
           - jnp.arange(LANES, dtype=jnp.int32)[None, :, None])
    bucket = _t5_bucket(rel).astype(jnp.int32)
    return pl.pallas_call(
        _bias_kernel,
        out_shape=(jax.ShapeDtypeStruct((DIFF_HEADS, N_BIAS_TILES, LANES, KV_TILE), f32),
                   jax.ShapeDtypeStruct((8, LANES), f32)),
        in_specs=[pl.BlockSpec(memory_space=pltpu.SMEM),
                  pl.BlockSpec(memory_space=pltpu.VMEM),
                  pl.BlockSpec(memory_space=pltpu.VMEM)],
        out_specs=(pl.BlockSpec(memory_space=pltpu.VMEM), pl.BlockSpec(memory_space=pltpu.VMEM)),
        name="rel_bias",
    )(rel_bias, bucket, lam_params)


def _diff_kernel(q_ref, k_ref, v_ref, bias_ref, lam_ref, nw_ref, o_ref, s_ref, e_ref, *, S, QB):
    qi = pl.program_id(2)
    n_tiles = S // KV_TILE
    nsub = QB // LANES
    dh = DIFF_HEAD_DIM
    q = (q_ref[0].astype(f32) * (dh ** -0.5)).astype(bf16)

    def tile(jt):
        return slice(jt * KV_TILE, (jt + 1) * KV_TILE)

    def score_tile(m, jt, mx):
        s = lax.dot_general(q[:, m * dh:(m + 1) * dh], k_ref[0, tile(jt), m * dh:(m + 1) * dh], NT_DIMS,
                            preferred_element_type=f32)
        parts = []
        for r in range(nsub):
            idx = jnp.clip(2 * jt - (qi * nsub + r), -3, 2) + 3
            parts.append(s[r * LANES:(r + 1) * LANES, :] + bias_ref[0, idx])
        s = jnp.concatenate(parts, axis=0) if nsub > 1 else parts[0]
        s_ref[m, :, tile(jt)] = s
        return jnp.maximum(mx, jnp.maximum(s[:, :LANES], s[:, LANES:]))

    def exp_tile(m, jt, row_max, den):
        e = jnp.exp(s_ref[m, :, tile(jt)] - row_max)
        e_ref[m, :, tile(jt)] = e.astype(bf16)
        return den + (e[:, :LANES] + e[:, LANES:])

    def pv_tile(m, jt, acc):
        return acc + jnp.dot(e_ref[m, :, tile(jt)], v_ref[0, tile(jt), :], preferred_element_type=f32)

    neg = jnp.full((QB, LANES), -jnp.inf, f32)
    zero = jnp.zeros((QB, LANES), f32)
    mx0 = neg
    for jt in range(n_tiles):
        mx0 = score_tile(0, jt, mx0)
    max0 = jnp.max(mx0, axis=-1, keepdims=True)
    mx1, den0 = neg, zero
    for jt in range(n_tiles):
        mx1 = score_tile(1, jt, mx1)
        den0 = exp_tile(0, jt, max0, den0)
    max1 = jnp.max(mx1, axis=-1, keepdims=True)
    pv0, den1 = jnp.zeros((QB, 2 * dh), f32), zero
    for jt in range(n_tiles):
        pv0 = pv_tile(0, jt, pv0)
        den1 = exp_tile(1, jt, max1, den1)
    pv1 = jnp.zeros((QB, 2 * dh), f32)
    for jt in range(n_tiles):
        pv1 = pv_tile(1, jt, pv1)
    outs = [pv0 / jnp.sum(den0, axis=-1, keepdims=True), pv1 / jnp.sum(den1, axis=-1, keepdims=True)]
    lam = lam_ref[0:1, 0:1]
    o = outs[0] - lam * outs[1]
    o = o * lax.rsqrt(jnp.mean(o * o, axis=-1, keepdims=True) + RMS_EPS) * nw_ref[...]
    o_ref[0] = (o * (1.0 - LAMBDA_INIT)).astype(bf16)


def _diff_attention(proj3, bias_tiles, lam, norm_w, *, qb):
    B, S, _ = proj3.shape
    w = 2 * DIFF_HEAD_DIM
    return pl.pallas_call(
        functools.partial(_diff_kernel, S=S, QB=qb),
        out_shape=jax.ShapeDtypeStruct((B, S, DIFF_HEADS * w), bf16),
        grid=(B, DIFF_HEADS, S // qb),
        in_specs=[pl.BlockSpec((1, qb, w), lambda b, h, i: (b, i, COL_DQ // w + h)),
                  pl.BlockSpec((1, S, w), lambda b, h, i: (b, 0, COL_DK // w + h)),
                  pl.BlockSpec((1, S, w), lambda b, h, i: (b, 0, COL_DV // w + h)),
                  pl.BlockSpec((1, N_BIAS_TILES, LANES, KV_TILE), lambda b, h, i: (h, 0, 0, 0)),
                  pl.BlockSpec((8, LANES), lambda b, h, i: (0, 0)),
                  pl.BlockSpec((1, w), lambda b, h, i: (0, 0))],
        out_specs=pl.BlockSpec((1, qb, w), lambda b, h, i: (b, i, h)),
        scratch_shapes=[pltpu.VMEM((2, qb, S), f32), pltpu.VMEM((2, qb, S), bf16)],
        compiler_params=_cparams(("parallel", "parallel", "arbitrary")),
        name="diff_attn",
    )(proj3, proj3, proj3, bias_tiles, lam, norm_w)


def _cross_kernel(q_ref, kv_ref, o_ref):
    w = CROSS_HEAD_DIM
    heads = range(CROSS_HEADS)
    s = [lax.dot_general(q_ref[0, :, h * w:(h + 1) * w], kv_ref[0, :, h * w:(h + 1) * w], NT_DIMS,
                         preferred_element_type=f32) * (w ** -0.5) for h in heads]
    e = [jnp.exp(s[h] - jnp.max(s[h], axis=-1, keepdims=True)) for h in heads]
    p = [(e[h] / jnp.sum(e[h], axis=-1, keepdims=True)).astype(bf16) for h in heads]
    for h in heads:
        o_ref[0, :, h * w:(h + 1) * w] = jnp.dot(
            p[h], kv_ref[0, :, (CROSS_HEADS + h) * w:(CROSS_HEADS + h + 1) * w],
            preferred_element_type=f32).astype(bf16)


def _cross_attention(proj3, kv3, *, qb):
    B, S, _ = proj3.shape
    M = kv3.shape[1]
    cw = CROSS_HEADS * CROSS_HEAD_DIM
    return pl.pallas_call(
        _cross_kernel,
        out_shape=jax.ShapeDtypeStruct((B, S, cw), bf16),
        grid=(B, S // qb),
        in_specs=[pl.BlockSpec((1, qb, cw), lambda b, i: (b, i, COL_CQ // cw)),
                  pl.BlockSpec((1, M, 2 * cw), lambda b, i: (b, 0, 0))],
        out_specs=pl.BlockSpec((1, qb, cw), lambda b, i: (b, i, 0)),
        compiler_params=_cparams(("parallel", "arbitrary")),
        name="cross_attn",
    )(proj3, kv3)


def _merge_kernel(yg_ref, yd_ref, yc_ref, g0_ref, g1_ref, g2_ref, wg_ref, wd_ref, wc_ref, o_ref):
    m = g0_ref[...].astype(f32) * jnp.dot(yg_ref[...], wg_ref[...], preferred_element_type=f32)
    m = m + g1_ref[...].astype(f32) * jnp.dot(yd_ref[...], wd_ref[...], preferred_element_type=f32)
    m = m + g2_ref[...].astype(f32) * jnp.dot(yc_ref[...], wc_ref[...], preferred_element_type=f32)
    o_ref[...] = m.astype(bf16)


def _merge(yg, yd, yc, proj2, wg, wd, wc, *, tm):
    T = yg.shape[0]
    gate_blk = COL_GATE // D_MODEL

    def yspec():
        return pl.BlockSpec((tm, GDN_WIDTH), lambda i: (i, 0))

    def gspec(n):
        return pl.BlockSpec((tm, D_MODEL), lambda i, n=n: (i, gate_blk + n))

    def wspec():
        return pl.BlockSpec((GDN_WIDTH, D_MODEL), lambda i: (0, 0), pipeline_mode=pl.Buffered(1))

    return pl.pallas_call(
        _merge_kernel,
        out_shape=jax.ShapeDtypeStruct((T, D_MODEL), bf16),
        grid=(T // tm,),
        in_specs=[yspec(), yspec(), yspec(), gspec(0), gspec(1), gspec(2), wspec(), wspec(), wspec()],
        out_specs=pl.BlockSpec((tm, D_MODEL), lambda i: (i, 0)),
        compiler_params=_cparams(("parallel",)),
        name="merge",
    )(yg, yd, yc, proj2, proj2, proj2, wg, wd, wc)


def _outproj_kernel(m_ref, x_ref, w_ref, g_ref, b_ref, o_ref):
    y = DEEPNORM_ALPHA * x_ref[...] + jnp.dot(m_ref[...], w_ref[...], preferred_element_type=f32)
    o_ref[...] = _layer_norm(y, g_ref[...], b_ref[...])


def _outproj_ln(merged, x2, w_out, g, b, *, tm):
    T = x2.shape[0]
    return pl.pallas_call(
        _outproj_kernel,
        out_shape=jax.ShapeDtypeStruct((T, D_MODEL), f32),
        grid=(T // tm,),
        in_specs=[pl.BlockSpec((tm, D_MODEL), lambda i: (i, 0)),
                  pl.BlockSpec((tm, D_MODEL), lambda i: (i, 0)),
                  pl.BlockSpec((D_MODEL, D_MODEL), lambda i: (0, 0), pipeline_mode=pl.Buffered(1)),
                  pl.BlockSpec((1, D_MODEL), lambda i: (0, 0)),
                  pl.BlockSpec((1, D_MODEL), lambda i: (0, 0))],
        out_specs=pl.BlockSpec((tm, D_MODEL), lambda i: (i, 0)),
        compiler_params=_cparams(("parallel",)),
        name="outproj_ln",
    )(merged, x2, w_out, g, b)


HALO = 8
FFN_TN = 512
FFN_SUB = 128
FFN_ROWS = 128


def _ffn_kernel(xm_ref, xp_ref, xn_ref, wg_ref, wv_ref, cg_ref, cv_ref, wdn_ref, g_ref, b_ref, o_ref,
                xh_ref, up_ref, h_ref, *, tm, tn, tiles_per_seq):
    i = pl.program_id(0)
    n = pl.program_id(1)

    @pl.when(n == 0)
    def _():
        pos = i % tiles_per_seq
        keep_prev = (pos != 0).astype(f32)
        keep_next = (pos != tiles_per_seq - 1).astype(f32)
        xh_ref[0:HALO, :] = (xp_ref[...] * keep_prev).astype(bf16)
        xh_ref[HALO:HALO + tm, :] = xm_ref[...].astype(bf16)
        xh_ref[HALO + tm:2 * HALO + tm, :] = (xn_ref[...] * keep_next).astype(bf16)
        o_ref[...] = DEEPNORM_ALPHA * xm_ref[...]

    taps = (cg_ref[...], cv_ref[...])
    planes = FFN_SUB // LANES

    def up_dot(c):
        cols = slice(c * FFN_SUB, (c + 1) * FFN_SUB)
        w = jnp.concatenate([wg_ref[:, cols], wv_ref[:, cols]], axis=1)
        up = jnp.dot(xh_ref[...], w, preferred_element_type=f32)
        for q in range(2 * planes):
            up_ref[c % 2, q] = up[:, q * LANES:(q + 1) * LANES]

    def act(c):
        for p in range(planes):
            c0 = c * FFN_SUB + p * LANES
            for r in range(tm // FFN_ROWS):
                r0 = r * FFN_ROWS
                conv = [sum(up_ref[c % 2, gv * planes + p, r0 + HALO - 1 + j:r0 + HALO - 1 + j + FFN_ROWS, :]
                            * taps[gv][j:j + 1, c0:c0 + LANES] for j in range(3)) for gv in range(2)]
                h_ref[r0:r0 + FFN_ROWS, c0:c0 + LANES] = (_silu(conv[0]) * conv[1]).astype(bf16)

    nsub = tn // FFN_SUB
    up_dot(0)
    for c in range(nsub):
        if c + 1 < nsub:
            up_dot(c + 1)
        act(c)
    o_ref[...] += jnp.dot(h_ref[...], wdn_ref[...], preferred_element_type=f32)

    @pl.when(n == pl.num_programs(1) - 1)
    def _():
        o_ref[...] = _layer_norm(o_ref[...], g_ref[...], b_ref[...])


def _ffn_ln(x1, w_up_p, conv_p, w_down_p, g, b, *, S, tm, tn):
    T = x1.shape[0]
    nf = D_FF_PAD // tn
    hpt = tm // HALO
    last_halo = T // HALO - 1
    return pl.pallas_call(
        functools.partial(_ffn_kernel, tm=tm, tn=tn, tiles_per_seq=S // tm),
        out_shape=jax.ShapeDtypeStruct((T, D_MODEL), f32),
        grid=(T // tm, nf),
        in_specs=[pl.BlockSpec((tm, D_MODEL), lambda i, n: (i, 0)),
                  pl.BlockSpec((HALO, D_MODEL), lambda i, n: (jnp.maximum(i * hpt - 1, 0), 0)),
                  pl.BlockSpec((HALO, D_MODEL), lambda i, n: (jnp.minimum((i + 1) * hpt, last_halo), 0)),
                  pl.BlockSpec((D_MODEL, tn), lambda i, n: (0, n)),
                  pl.BlockSpec((D_MODEL, tn), lambda i, n: (0, nf + n)),
                  pl.BlockSpec((3, tn), lambda i, n: (0, n)),
                  pl.BlockSpec((3, tn), lambda i, n: (0, nf + n)),
                  pl.BlockSpec((tn, D_MODEL), lambda i, n: (n, 0)),
                  pl.BlockSpec((1, D_MODEL), lambda i, n: (0, 0)),
                  pl.BlockSpec((1, D_MODEL), lambda i, n: (0, 0))],
        out_specs=pl.BlockSpec((tm, D_MODEL), lambda i, n: (i, 0)),
        scratch_shapes=[pltpu.VMEM((tm + 2 * HALO, D_MODEL), bf16),
                        pltpu.VMEM((2, 2 * FFN_SUB // LANES, tm + 2 * HALO, LANES), f32),
                        pltpu.VMEM((tm, tn), bf16)],
        compiler_params=_cparams(("parallel", "arbitrary")),
        name="ffn_ln",
    )(x1, x1, x1, w_up_p, w_up_p, conv_p, conv_p, w_down_p, g, b)


def _pick(n, prefs):
    for p in prefs:
        if n % p == 0:
            return p
    raise ValueError(f"no tile in {prefs} divides {n}")


def _trunk(x, mem, wts):
    B, S, _ = x.shape
    T = B * S
    assert S % KV_TILE == 0 and (S // SB) % GDN_UNROLL == 0
    x2 = x.reshape(T, D_MODEL)

    proj2, ab2 = _inproj(x2, wts["w_all"], wts["b_all"], wts["w_ab"], tm=_pick(T, (1024, 512, 256)), tn=1024)
    proj3 = proj2.reshape(B, S, PROJ_COLS)
    ab3 = ab2.reshape(B, S, LANES)

    y_gdn = _gdn(proj3, ab3, wts["gdn_conv"], wts["a_log"], wts["dt_bias"], wts["gdn_norm_w"])
    y_diff = _diff_attention(proj3, wts["bias_tiles"], wts["lam"], wts["diff_norm_w"], qb=_pick(S, (512, 256)))
    M = mem.shape[1]
    kv = _matmul(mem.reshape(B * M, D_MODEL), wts["w_mem_kv"], tm=_pick(B * M, (512, 256)))
    y_cross = _cross_attention(proj3, kv.reshape(B, M, 2 * CROSS_HEADS * CROSS_HEAD_DIM), qb=_pick(S, (1024, 512, 256)))

    tm = _pick(S, (512, 256))
    merged = _merge(y_gdn.reshape(T, -1), y_diff.reshape(T, -1), y_cross.reshape(T, -1), proj2,
                    wts["w_bg"], wts["w_bd"], wts["w_bc"], tm=tm)
    x1 = _outproj_ln(merged, x2, wts["w_out"], wts["ln1_g"], wts["ln1_b"], tm=tm)
    y = _ffn_ln(x1, wts["w_up_p"], wts["ffn_conv_p"], wts["w_down_p"], wts["ln2_g"], wts["ln2_b"],
                S=S, tm=tm, tn=FFN_TN)
    return y.reshape(B, S, D_MODEL)


def _prep_weights(rel_bias, w_in, gdn_conv, gdn_a_log, gdn_dt_bias, gdn_norm_w, diff_lambda, diff_norm_w,
                  w_mem_kv, w_gate, b_gate, w_branch_gdn, w_branch_diff, w_branch_cross, w_out,
                  ln1_g, ln1_b, w_up, ffn_conv, w_down, ln2_g, ln2_b):
    l = 0
    wi = w_in[l]
    a0 = 4 * GDN_WIDTH
    a1 = a0 + 4 * GDN_HEADS
    w_lin = jnp.concatenate([wi[:, :a0], wi[:, a1:]], axis=1)
    w_all = jnp.concatenate([w_lin, w_gate[l]], axis=1).astype(bf16)
    b_all = jnp.concatenate([jnp.zeros((PROJ_LIN,), f32), b_gate[l].astype(f32)])[None, :]
    w_ab = jnp.pad(wi[:, a0:a1], ((0, 0), (0, LANES - 4 * GDN_HEADS))).astype(bf16)

    padc = D_FF_PAD - D_FF

    def pad_halves(a):
        zeros = jnp.zeros((a.shape[0], padc), a.dtype)
        return jnp.concatenate([a[:, :D_FF], zeros, a[:, D_FF:], zeros], axis=1)

    bias_tiles, lam = _bias_tiles(rel_bias.astype(f32), diff_lambda[l].astype(f32))
    return dict(
        w_all=w_all, b_all=b_all, w_ab=w_ab,
        gdn_conv=gdn_conv[l].astype(f32), a_log=gdn_a_log[l].astype(f32), dt_bias=gdn_dt_bias[l].astype(f32),
        gdn_norm_w=gdn_norm_w[l].astype(f32)[None, :],
        bias_tiles=bias_tiles, lam=lam, diff_norm_w=diff_norm_w[l].astype(f32)[None, :],
        w_mem_kv=w_mem_kv[l].astype(bf16),
        w_bg=w_branch_gdn[l].astype(bf16), w_bd=w_branch_diff[l].astype(bf16), w_bc=w_branch_cross[l].astype(bf16),
        w_out=w_out[l].astype(bf16),
        ln1_g=ln1_g[l].astype(f32)[None, :], ln1_b=ln1_b[l].astype(f32)[None, :],
        w_up_p=pad_halves(w_up[l].astype(bf16)), ffn_conv_p=pad_halves(ffn_conv[l].astype(f32)),
        w_down_p=jnp.pad(w_down[l], ((0, padc), (0, 0))).astype(bf16),
        ln2_g=ln2_g[l].astype(f32)[None, :], ln2_b=ln2_b[l].astype(f32)[None, :],
    )


def kernel(x_prompt, x_sample, mem_prompt, mem_sample, rel_bias, w_in, gdn_conv, gdn_a_log, gdn_dt_bias, gdn_norm_w, diff_lambda, diff_norm_w, w_mem_kv, w_gate, b_gate, w_branch_gdn, w_branch_diff, w_branch_cross, w_out, ln1_g, ln1_b, w_up, ffn_conv, w_down, ln2_g, ln2_b):
    wts = _prep_weights(rel_bias, w_in, gdn_conv, gdn_a_log, gdn_dt_bias, gdn_norm_w, diff_lambda, diff_norm_w,
                        w_mem_kv, w_gate, b_gate, w_branch_gdn, w_branch_diff, w_branch_cross, w_out,
                        ln1_g, ln1_b, w_up, ffn_conv, w_down, ln2_g, ln2_b)
    return (_trunk(x_prompt, mem_prompt, wts), _trunk(x_sample, mem_sample, wts))
```

```python
import functools
import math

import jax
import jax.numpy as jnp
import numpy as np
from jax import lax
from jax.experimental import pallas as pl
from jax.experimental.pallas import tpu as pltpu

f32 = jnp.float32
bf16 = jnp.bfloat16

D_MODEL = 2048
GDN_HEADS = 8
GDN_HEAD_DIM = 128
GDN_WIDTH = 1024
GDN_CONV = 5
GDN_CHUNK = 64
DIFF_HEADS = 4
DIFF_HEAD_DIM = 128
CROSS_HEADS = 4
CROSS_HEAD_DIM = 256
N_BRANCH = 3
D_FF = 5504
NUM_BUCKETS = 32
MAX_DISTANCE = 128
LN_EPS = 1e-5
RMS_EPS = 1e-6
L2_EPS = 1e-6
DEPTH = 1
DEEPNORM_ALPHA = (2 * DEPTH) ** 0.25
LAMBDA_INIT = 0.8 - 0.6 * math.exp(-0.3 * 0)

LANES = 128
MXU_WIDTH = 256
VMEM_LIMIT_BYTES = 56 * 1024 * 1024

PROJ_LIN = 8192
COL_GQ, COL_GK, COL_GV, COL_GZ = 0, 1024, 2048, 3072
COL_DQ, COL_DK, COL_DV, COL_CQ = 4096, 5120, 6144, 7168
COL_GATE = PROJ_LIN
PROJ_COLS = PROJ_LIN + N_BRANCH * D_MODEL
D_FF_PAD = 5632

NT_DIMS = (((1,), (1,)), ((), ()))


def _cparams(sem, vmem=VMEM_LIMIT_BYTES):
    return pltpu.CompilerParams(dimension_semantics=sem, vmem_limit_bytes=vmem)


def _sigmoid(x):
    return 0.5 * jnp.tanh(0.5 * x) + 0.5


def _silu(x):
    return x * _sigmoid(x)


def _layer_norm(y, g, b):
    mu = jnp.mean(y, axis=-1, keepdims=True)
    yc = y - mu
    var = jnp.mean(yc * yc, axis=-1, keepdims=True)
    return yc * lax.rsqrt(var + LN_EPS) * g + b


def _bdot(a, b):
    return jnp.dot(a.astype(bf16), b.astype(bf16), preferred_element_type=f32)


def _inproj_kernel(x_ref, w_ref, b_ref, wab_ref, o_ref, ab_ref, xb_ref, *, n_lin_tiles):
    j = pl.program_id(1)

    @pl.when(j == 0)
    def _():
        xb = x_ref[...].astype(bf16)
        xb_ref[...] = xb
        ab_ref[...] = jnp.dot(xb, wab_ref[...], preferred_element_type=f32)

    acc = jnp.dot(xb_ref[...], w_ref[...], preferred_element_type=f32)
    o_ref[...] = jnp.where(j >= n_lin_tiles, _sigmoid(acc + b_ref[...]), acc).astype(bf16)


def _inproj(x2, w_all, b_all, w_ab, *, tm, tn):
    T = x2.shape[0]
    n_lin_tiles = PROJ_LIN // tn
    return pl.pallas_call(
        functools.partial(_inproj_kernel, n_lin_tiles=n_lin_tiles),
        out_shape=(jax.ShapeDtypeStruct((T, PROJ_COLS), bf16),
                   jax.ShapeDtypeStruct((T, LANES), f32)),
        grid=(T // tm, PROJ_COLS // tn),
        in_specs=[pl.BlockSpec((tm, D_MODEL), lambda i, j: (i, 0)),
                  pl.BlockSpec((D_MODEL, tn), lambda i, j: (0, j)),
                  pl.BlockSpec((1, tn), lambda i, j: (0, j)),
                  pl.BlockSpec((D_MODEL, LANES), lambda i, j: (0, 0))],
        out_specs=(pl.BlockSpec((tm, tn), lambda i, j: (i, j)),
                   pl.BlockSpec((tm, LANES), lambda i, j: (i, 0))),
        scratch_shapes=[pltpu.VMEM((tm, D_MODEL), bf16)],
        compiler_params=_cparams(("parallel", "arbitrary")),
        name="inproj",
    )(x2, w_all, b_all, w_ab)


def _mm_kernel(x_ref, w_ref, o_ref):
    o_ref[...] = jnp.dot(x_ref[...].astype(bf16), w_ref[...], preferred_element_type=f32).astype(o_ref.dtype)


def _matmul(x2, w, *, tm, out_dtype=bf16):
    T, K = x2.shape
    N = w.shape[1]
    return pl.pallas_call(
        _mm_kernel,
        out_shape=jax.ShapeDtypeStruct((T, N), out_dtype),
        grid=(T // tm,),
        in_specs=[pl.BlockSpec((tm, K), lambda i: (i, 0)),
                  pl.BlockSpec((K, N), lambda i: (0, 0), pipeline_mode=pl.Buffered(1))],
        out_specs=pl.BlockSpec((tm, N), lambda i: (i, 0)),
        compiler_params=_cparams(("parallel",)),
        name="mem_kv",
    )(x2, w)


SB = 2 * GDN_CHUNK
CONV_PAD = 8
GDN_UNROLL = 8
INV_BASE = 8


def _split_bf16(x):
    hi = x.astype(bf16)
    lo = (x - hi.astype(f32)).astype(bf16)
    return hi, lo


def _gdn_kernel(alog_ref, dtb_ref, q_ref, k_ref, v_ref, z_ref, ab_ref, cq_ref, ck_ref, cv_ref,
                nw_ref, o_ref, xp_ref, lhs_ref, n_ref, egl_ref, oacc_ref, *, S, unroll):
    h = pl.program_id(1)
    nsb = S // SB
    hd = GDN_HEAD_DIM

    zpad = jnp.zeros((CONV_PAD, hd), f32)
    for si, src_ref in enumerate((q_ref, k_ref, v_ref)):
        xp_ref[si, 0:CONV_PAD, :] = zpad
        xp_ref[si, S + CONV_PAD:S + 2 * CONV_PAD, :] = zpad
        xp_ref[si, CONV_PAD:S + CONV_PAD, :] = src_ref[0].astype(f32)
    taps = (cq_ref[...], ck_ref[...], cv_ref[...])

    def conv_silu(si, sb):
        r0 = pl.multiple_of(sb * SB, SB)
        acc = jnp.zeros((SB, hd), f32)
        for j in range(GDN_CONV):
            acc = acc + xp_ref[si, pl.ds(r0 + CONV_PAD - GDN_CONV // 2 + j, SB), :] * taps[si][j:j + 1, :]
        y = _silu(acc)
        if si < 2:
            y = y * lax.rsqrt(jnp.sum(y * y, axis=-1, keepdims=True) + L2_EPS)
        if si == 0:
            y = y * (hd ** -0.5)
        return y

    ri = lax.broadcasted_iota(jnp.int32, (SB, SB), 0)
    ci = lax.broadcasted_iota(jnp.int32, (SB, SB), 1)

    def same(blk):
        return (ri // blk) == (ci // blk)

    same_chunk = same(GDN_CHUNK)
    incl = (same_chunk & (ri >= ci), same_chunk & (ri <= ci))
    strict = (same_chunk & (ri > ci), same_chunk & (ri < ci))
    same_base = same(INV_BASE)
    merge_masks = []
    blk = INV_BASE
    while blk < GDN_CHUNK:
        merge_masks.append(same(2 * blk) & jnp.logical_not(same(blk)))
        blk *= 2
    eye = (ri == ci).astype(f32)
    tri2 = [jnp.concatenate([incl[d].astype(bf16)] * 2, axis=1) for d in range(2)]
    r4 = lax.broadcasted_iota(jnp.int32, (2 * SB, 4 * hd), 0) % SB
    c4 = lax.broadcasted_iota(jnp.int32, (2 * SB, 4 * hd), 1) // hd
    sel = (r4 == (c4 * GDN_HEADS + h)).astype(bf16)
    is_a_col = lax.broadcasted_iota(jnp.int32, (SB, hd), 1) < 2 * GDN_HEADS
    rh = lax.broadcasted_iota(jnp.int32, (SB, 4 * hd), 0) < GDN_CHUNK
    ch = lax.broadcasted_iota(jnp.int32, (SB, 4 * hd), 1) < 2 * hd
    half_mask = rh == ch

    ends = unroll // 2
    n_groups = nsb // unroll

    def chunk_local(i, rec=()):
        rec = list(rec)

        def run_rec():
            if rec:
                rec.pop(0)()

        chains = []
        for u in range(unroll):
            sb = i * ends + u if u < ends else nsb - (i + 1) * ends + (u - ends)
            rows = pl.ds(pl.multiple_of(sb * SB, SB), SB)
            qn, kn, vn = (conv_silu(si, sb) for si in range(3))
            knb = kn.astype(bf16)
            kq = lax.dot_general(jnp.concatenate([knb, qn.astype(bf16)], axis=0), knb, NT_DIMS,
                                 preferred_element_type=f32)
            ab = ab_ref[0, rows, :]
            xs = ab + dtb_ref[...]
            softplus = jnp.maximum(xs, 0.0) + jnp.log1p(jnp.exp(-jnp.abs(xs)))
            gates = jnp.where(is_a_col, -jnp.exp(alog_ref[...]) * softplus, _sigmoid(ab))
            g_hi, g_lo = _split_bf16(gates)
            gsel = jnp.dot(jnp.concatenate([g_hi, g_lo], axis=1), sel, preferred_element_type=f32)
            for d in range(2):
                chains.append(dict(sb=sb, rows=rows, d=d, kn=kn, qn=qn, vn=vn, kk=kq[:SB], qk=kq[SB:],
                                   g=gsel[:, d * hd:(d + 1) * hd], beta=gsel[:, (2 + d) * hd:(3 + d) * hd]))
        for t in chains:
            d = t["d"]
            g_hi, g_lo = _split_bf16(t["g"])
            t["gc"] = jnp.dot(tri2[d], jnp.concatenate([g_hi, g_lo], axis=0),
                              preferred_element_type=f32)
        run_rec()
        for t in chains:
            d, gc, beta = t["d"], t["gc"], t["beta"]
            diff = gc - gc.T
            t["decay"] = jnp.where(incl[d], jnp.exp(jnp.where(incl[d], diff, 0.0)), 0.0)
            t["l"] = jnp.where(strict[d], beta * t["kk"] * t["decay"], 0.0)
            t["egc"] = jnp.exp(gc)
            t["rhs"] = jnp.concatenate([beta * t["vn"], (beta * t["egc"]) * t["kn"]], axis=1)
            t["ld"] = jnp.where(same_base, t["l"], 0.0)
            t["ld2"] = _bdot(t["ld"], t["ld"])
        run_rec()
        for t in chains:
            iml = eye - t["ld"]
            t["x"] = iml + _bdot(iml, t["ld2"])
            t["ld4"] = _bdot(t["ld2"], t["ld2"])
        run_rec()
        for t in chains:
            t["t"] = t["x"] + _bdot(t["x"], t["ld4"])
        run_rec()
        for mask in merge_masks:
            for t in chains:
                t["w"] = _bdot(t["t"], jnp.where(mask, t["l"], 0.0))
            run_rec()
            for t in chains:
                t["t"] = t["t"] - _bdot(t["w"], t["t"])
            run_rec()
        for t in chains:
            t["sol"] = _bdot(t["t"], t["rhs"])
        while rec:
            run_rec()
        for t in chains:
            d, gc, sol = t["d"], t["gc"], t["sol"]
            w = jnp.concatenate([sol[:, hd:], sol[:, :hd]], axis=1).astype(bf16)
            qkd = jnp.where(incl[d], t["qk"] * t["decay"], 0.0).astype(bf16)
            t["qw"] = jnp.dot(qkd, w, preferred_element_type=f32)
            last = (GDN_CHUNK - 1) if d == 0 else 0
            gl = jnp.concatenate(
                [jnp.broadcast_to(gc[last:last + 1, :], (GDN_CHUNK, hd)),
                 jnp.broadcast_to(gc[GDN_CHUNK + last:GDN_CHUNK + last + 1, :], (GDN_CHUNK, hd))], axis=0)
            kdt = (t["kn"] * jnp.exp(gl - gc)).T.astype(bf16)
            wpair = jnp.where(half_mask, jnp.concatenate([w, w], axis=1), jnp.zeros((SB, 4 * hd), bf16))
            t["kw"] = jnp.dot(kdt, wpair, preferred_element_type=f32)
            t["egl"] = jnp.exp(gl)
        for t in chains:
            d = t["d"]
            f_, s_ = (0, 1) if d == 0 else (1, 0)
            kw = t["kw"]
            t["qp"] = t["qn"] * t["egc"] - t["qw"][:, :hd]
            a = [-kw[:, 2 * hf * hd:(2 * hf + 1) * hd] for hf in range(2)]
            nn = [kw[:, (2 * hf + 1) * hd:(2 * hf + 2) * hd] for hf in range(2)]
            qs = t["qp"][s_ * GDN_CHUNK:(s_ + 1) * GDN_CHUNK, :]
            t["am"], t["nm"] = a, nn
            t["xc"] = _bdot(jnp.concatenate([a[s_], qs], axis=0), jnp.concatenate([a[f_], nn[f_]], axis=1))
        for t in chains:
            d, sb = t["d"], t["sb"]
            f_, s_ = (0, 1) if d == 0 else (1, 0)
            a, nn, x, qp = t["am"], t["nm"], t["xc"], t["qp"]
            dec = [t["egl"][hf * GDN_CHUNK:hf * GDN_CHUNK + 1, :] for hf in range(2)]
            a2 = dec[s_] * a[f_] + dec[f_] * a[s_] + x[:hd, :hd]
            n2 = dec[s_] * nn[f_] + x[:hd, hd:] + nn[s_]
            q_f = qp[f_ * GDN_CHUNK:(f_ + 1) * GDN_CHUNK, :]
            q_s = dec[f_] * qp[s_ * GDN_CHUNK:(s_ + 1) * GDN_CHUNK, :] + x[hd:, :hd]
            q2 = jnp.concatenate([q_f, q_s] if d == 0 else [q_s, q_f], axis=0)
            zero = jnp.zeros((GDN_CHUNK, hd), f32)
            t["o"] = t["qw"][:, hd:] + jnp.concatenate([zero, x[hd:, hd:]] if d == 0 else [x[hd:, hd:], zero], axis=0)
            l0 = pl.multiple_of(sb * 2 * SB, 2 * SB)
            lhs_ref[d, pl.ds(l0, SB), :] = a2.astype(bf16)
            lhs_ref[d, pl.ds(l0 + SB, SB), :] = q2.astype(bf16)
            n_ref[d, pl.ds(pl.multiple_of(sb * SB, SB), SB), :] = n2
            egl_ref[d, pl.ds(pl.multiple_of(sb * 8, 8), 8), :] = jnp.broadcast_to(dec[0] * dec[1], (8, hd))
        for tf, tb in zip(chains[0::2], chains[1::2]):
            oacc_ref[tf["rows"], :] = tf["o"] + tb["o"]

    def sb_step(d, sbi, state):
        lhs = lhs_ref[d, pl.ds(pl.multiple_of(sbi * 2 * SB, 2 * SB), 2 * SB), :]
        pop = jnp.dot(lhs, state.astype(bf16), preferred_element_type=f32)
        rows = pl.ds(pl.multiple_of(sbi * SB, SB), SB)
        oacc_ref[rows, :] += pop[SB:, :]
        dec = egl_ref[d, pl.ds(pl.multiple_of(sbi * 8, 8), 8), :][0:1, :]
        return dec * state + pop[:SB, :] + n_ref[d, rows, :]

    def rec_steps(g, states):
        def step(j):
            def run():
                sf, sbk = states[0]
                t = g * ends + j
                states[0] = (sb_step(0, t, sf), sb_step(1, nsb - 1 - t, sbk))
            return run
        return [step(j) for j in range(ends)]

    z0 = jnp.zeros((hd, hd), f32)
    chunk_local(0)

    def piped(i, carry):
        states = [carry]
        chunk_local(i, rec_steps(i - 1, states))
        return states[0]

    carry = lax.fori_loop(1, n_groups, piped, (z0, z0))
    states = [carry]
    for run in rec_steps(n_groups - 1, states):
        run()

    nw = nw_ref[...]

    def finish(sbi):
        rows = pl.ds(pl.multiple_of(sbi * SB, SB), SB)
        o = oacc_ref[rows, :]
        o = o * lax.rsqrt(jnp.mean(o * o, axis=-1, keepdims=True) + RMS_EPS) * nw
        o_ref[0, rows, :] = (o * _silu(z_ref[0, rows, :].astype(f32))).astype(bf16)

    def tail(t, carry):
        finish(t - 1)
        finish(nsb - t)
        sf, sbk = carry
        return sb_step(0, t, sf), sb_step(1, nsb - 1 - t, sbk)

    t0 = nsb // 2
    sf, sbk = states[0]
    carry = sb_step(0, t0, sf), sb_step(1, nsb - 1 - t0, sbk)
    lax.fori_loop(t0 + 1, nsb, tail, carry)
    finish(nsb - 1)
    finish(0)


def _gdn(proj3, ab3, conv_w, a_log, dt_bias, norm_w):
    B, S, _ = proj3.shape
    hb = GDN_HEAD_DIM
    nsb = S // SB

    def col(base):
        return pl.BlockSpec((1, S, hb), lambda b, h, base=base: (b, 0, base // hb + h))

    def cw(base):
        return pl.BlockSpec((GDN_CONV, hb), lambda b, h, base=base: (0, base // hb + h))

    lane_vec = pl.BlockSpec((1, LANES), lambda b, h: (0, 0))
    return pl.pallas_call(
        functools.partial(_gdn_kernel, S=S, unroll=GDN_UNROLL),
        out_shape=jax.ShapeDtypeStruct((B, S, GDN_WIDTH), bf16),
        grid=(B, GDN_HEADS),
        in_specs=[lane_vec, lane_vec, col(COL_GQ), col(COL_GK), col(COL_GV), col(COL_GZ),
                  pl.BlockSpec((1, S, LANES), lambda b, h: (b, 0, 0)),
                  cw(0), cw(GDN_WIDTH), cw(2 * GDN_WIDTH),
                  pl.BlockSpec((1, hb), lambda b, h: (0, 0))],
        out_specs=pl.BlockSpec((1, S, hb), lambda b, h: (b, 0, h)),
        scratch_shapes=[pltpu.VMEM((3, S + 2 * CONV_PAD, hb), f32),
                        pltpu.VMEM((2, nsb * 2 * SB, hb), bf16),
                        pltpu.VMEM((2, nsb * SB, hb), f32),
                        pltpu.VMEM((2, nsb * 8, hb), f32),
                        pltpu.VMEM((S, hb), f32)],
        compiler_params=_cparams(("parallel", "arbitrary")),
        name="gdn",
    )(a_log, dt_bias, proj3, proj3, proj3, proj3, ab3, conv_w, conv_w, conv_w, norm_w)


N_BIAS_TILES = 6
KV_TILE = 2 * LANES


def _bias_kernel(rb_ref, bucket_ref, lp_ref, bias_ref, lam_ref):
    bucket = bucket_ref[...]
    for hh in range(DIFF_HEADS):
        acc = jnp.zeros(bucket.shape, f32)
        for bk in range(NUM_BUCKETS):
            acc = jnp.where(bucket == bk, rb_ref[bk, hh], acc)
        bias_ref[hh] = acc
    lp = lp_ref[...]
    lam = (jnp.exp(jnp.sum(lp[0:1] * lp[1:2], axis=-1, keepdims=True))
           - jnp.exp(jnp.sum(lp[2:3] * lp[3:4], axis=-1, keepdims=True)) + LAMBDA_INIT)
    lam_ref[...] = jnp.broadcast_to(lam, lam_ref.shape)


def _t5_bucket(rel):
    nb = NUM_BUCKETS // 2
    max_exact = nb // 2
    ret = jnp.where(rel > 0, nb, 0)
    n = jnp.abs(rel)
    nf = jnp.maximum(n, 1).astype(jnp.float32)
    large = max_exact + (jnp.log(nf / max_exact) / math.log(MAX_DISTANCE / max_exact)
                         * (nb - max_exact)).astype(jnp.int32)
    large = jnp.minimum(large, nb - 1)
    return ret + jnp.where(n < max_exact, n, large)


def _bias_tiles(rel_bias, lam_params):
    delta = jnp.arange(N_BIAS_TILES, dtype=jnp.int32)[:, None, None] - 3
    rel = (LANES * delta + jnp.arange(KV_TILE, dtype=jnp.int32)[None, None, :]
           - jnp.arange(LANES, dtype=jnp.int32)[None, :, None])
    bucket = _t5_bucket(rel).astype(jnp.int32)
    return pl.pallas_call(
        _bias_kernel,
        out_shape=(jax.ShapeDtypeStruct((DIFF_HEADS, N_BIAS_TILES, LANES, KV_TILE), f32),
                   jax.ShapeDtypeStruct((8, LANES), f32)),
        in_specs=[pl.BlockSpec(memory_space=pltpu.SMEM),
                  pl.BlockSpec(memory_space=pltpu.VMEM),
                  pl.BlockSpec(memory_space=pltpu.VMEM)],
        out_specs=(pl.BlockSpec(memory_space=pltpu.VMEM), pl.BlockSpec(memory_space=pltpu.VMEM)),
        name="rel_bias",
    )(rel_bias, bucket, lam_params)


def _diff_kernel(q_ref, k_ref, v_ref, bias_ref, lam_ref, nw_ref, o_ref, s_ref, e_ref, *, S, QB):
    qi = pl.program_id(2)
    n_tiles = S // KV_TILE
    nsub = QB // LANES
    dh = DIFF_HEAD_DIM
    q = (q_ref[0].astype(f32) * (dh ** -0.5)).astype(bf16)

    def tile(jt):
        return slice(jt * KV_TILE, (jt + 1) * KV_TILE)

    def score_tile(m, jt, mx):
        s = lax.dot_general(q[:, m * dh:(m + 1) * dh], k_ref[0, tile(jt), m * dh:(m + 1) * dh], NT_DIMS,
                            preferred_element_type=f32)
        parts = []
        for r in range(nsub):
            idx = jnp.clip(2 * jt - (qi * nsub + r), -3, 2) + 3
            parts.append(s[r * LANES:(r + 1) * LANES, :] + bias_ref[0, idx])
        s = jnp.concatenate(parts, axis=0) if nsub > 1 else parts[0]
        s_ref[m, :, tile(jt)] = s
        return jnp.maximum(mx, jnp.maximum(s[:, :LANES], s[:, LANES:]))

    def exp_tile(m, jt, row_max, den):
        e = jnp.exp(s_ref[m, :, tile(jt)] - row_max)
        e_ref[m, :, tile(jt)] = e.astype(bf16)
        return den + (e[:, :LANES] + e[:, LANES:])

    def pv_tile(m, jt, acc):
        return acc + jnp.dot(e_ref[m, :, tile(jt)], v_ref[0, tile(jt), :], preferred_element_type=f32)

    neg = jnp.full((QB, LANES), -jnp.inf, f32)
    zero = jnp.zeros((QB, LANES), f32)
    mx0 = neg
    for jt in range(n_tiles):
        mx0 = score_tile(0, jt, mx0)
    max0 = jnp.max(mx0, axis=-1, keepdims=True)
    mx1, den0 = neg, zero
    for jt in range(n_tiles):
        mx1 = score_tile(1, jt, mx1)
        den0 = exp_tile(0, jt, max0, den0)
    max1 = jnp.max(mx1, axis=-1, keepdims=True)
    pv0, den1 = jnp.zeros((QB, 2 * dh), f32), zero
    for jt in range(n_tiles):
        pv0 = pv_tile(0, jt, pv0)
        den1 = exp_tile(1, jt, max1, den1)
    pv1 = jnp.zeros((QB, 2 * dh), f32)
    for jt in range(n_tiles):
        pv1 = pv_tile(1, jt, pv1)
    outs = [pv0 / jnp.sum(den0, axis=-1, keepdims=True), pv1 / jnp.sum(den1, axis=-1, keepdims=True)]
    lam = lam_ref[0:1, 0:1]
    o = outs[0] - lam * outs[1]
    o = o * lax.rsqrt(jnp.mean(o * o, axis=-1, keepdims=True) + RMS_EPS) * nw_ref[...]
    o_ref[0] = (o * (1.0 - LAMBDA_INIT)).astype(bf16)


def _diff_attention(proj3, bias_tiles, lam, norm_w, *, qb):
    B, S, _ = proj3.shape
    w = 2 * DIFF_HEAD_DIM
    return pl.pallas_call(
        functools.partial(_diff_kernel, S=S, QB=qb),
        out_shape=jax.ShapeDtypeStruct((B, S, DIFF_HEADS * w), bf16),
        grid=(B, DIFF_HEADS, S // qb),
        in_specs=[pl.BlockSpec((1, qb, w), lambda b, h, i: (b, i, COL_DQ // w + h)),
                  pl.BlockSpec((1, S, w), lambda b, h, i: (b, 0, COL_DK // w + h)),
                  pl.BlockSpec((1, S, w), lambda b, h, i: (b, 0, COL_DV // w + h)),
                  pl.BlockSpec((1, N_BIAS_TILES, LANES, KV_TILE), lambda b, h, i: (h, 0, 0, 0)),
                  pl.BlockSpec((8, LANES), lambda b, h, i: (0, 0)),
                  pl.BlockSpec((1, w), lambda b, h, i: (0, 0))],
        out_specs=pl.BlockSpec((1, qb, w), lambda b, h, i: (b, i, h)),
        scratch_shapes=[pltpu.VMEM((2, qb, S), f32), pltpu.VMEM((2, qb, S), bf16)],
        compiler_params=_cparams(("parallel", "parallel", "arbitrary")),
        name="diff_attn",
    )(proj3, proj3, proj3, bias_tiles, lam, norm_w)


def _cross_kernel(q_ref, kv_ref, o_ref):
    w = CROSS_HEAD_DIM
    heads = range(CROSS_HEADS)
    s = [lax.dot_general(q_ref[0, :, h * w:(h + 1) * w], kv_ref[0, :, h * w:(h + 1) * w], NT_DIMS,
                         preferred_element_type=f32) * (w ** -0.5) for h in heads]
    e = [jnp.exp(s[h] - jnp.max(s[h], axis=-1, keepdims=True)) for h in heads]
    p = [(e[h] / jnp.sum(e[h], axis=-1, keepdims=True)).astype(bf16) for h in heads]
    for h in heads:
        o_ref[0, :, h * w:(h + 1) * w] = jnp.dot(
            p[h], kv_ref[0, :, (CROSS_HEADS + h) * w:(CROSS_HEADS + h + 1) * w],
            preferred_element_type=f32).astype(bf16)


def _cross_attention(proj3, kv3, *, qb):
    B, S, _ = proj3.shape
    M = kv3.shape[1]
    cw = CROSS_HEADS * CROSS_HEAD_DIM
    return pl.pallas_call(
        _cross_kernel,
        out_shape=jax.ShapeDtypeStruct((B, S, cw), bf16),
        grid=(B, S // qb),
        in_specs=[pl.BlockSpec((1, qb, cw), lambda b, i: (b, i, COL_CQ // cw)),
                  pl.BlockSpec((1, M, 2 * cw), lambda b, i: (b, 0, 0))],
        out_specs=pl.BlockSpec((1, qb, cw), lambda b, i: (b, i, 0)),
        compiler_params=_cparams(("parallel", "arbitrary")),
        name="cross_attn",
    )(proj3, kv3)


def _merge_kernel(yg_ref, yd_ref, yc_ref, g0_ref, g1_ref, g2_ref, wg_ref, wd_ref, wc_ref, o_ref):
    m = g0_ref[...].astype(f32) * jnp.dot(yg_ref[...], wg_ref[...], preferred_element_type=f32)
    m = m + g1_ref[...].astype(f32) * jnp.dot(yd_ref[...], wd_ref[...], preferred_element_type=f32)
    m = m + g2_ref[...].astype(f32) * jnp.dot(yc_ref[...], wc_ref[...], preferred_element_type=f32)
    o_ref[...] = m.astype(bf16)


def _merge(yg, yd, yc, proj2, wg, wd, wc, *, tm):
    T = yg.shape[0]
    gate_blk = COL_GATE // D_MODEL

    def yspec():
        return pl.BlockSpec((tm, GDN_WIDTH), lambda i: (i, 0))

    def gspec(n):
        return pl.BlockSpec((tm, D_MODEL), lambda i, n=n: (i, gate_blk + n))

    def wspec():
        return pl.BlockSpec((GDN_WIDTH, D_MODEL), lambda i: (0, 0), pipeline_mode=pl.Buffered(1))

    return pl.pallas_call(
        _merge_kernel,
        out_shape=jax.ShapeDtypeStruct((T, D_MODEL), bf16),
        grid=(T // tm,),
        in_specs=[yspec(), yspec(), yspec(), gspec(0), gspec(1), gspec(2), wspec(), wspec(), wspec()],
        out_specs=pl.BlockSpec((tm, D_MODEL), lambda i: (i, 0)),
        compiler_params=_cparams(("parallel",)),
        name="merge",
    )(yg, yd, yc, proj2, proj2, proj2, wg, wd, wc)


def _outproj_kernel(m_ref, x_ref, w_ref, g_ref, b_ref, o_ref):
    y = DEEPNORM_ALPHA * x_ref[...] + jnp.dot(m_ref[...], w_ref[...], preferred_element_type=f32)
    o_ref[...] = _layer_norm(y, g_ref[...], b_ref[...])


def _outproj_ln(merged, x2, w_out, g, b, *, tm):
    T = x2.shape[0]
    return pl.pallas_call(
        _outproj_kernel,
        out_shape=jax.ShapeDtypeStruct((T, D_MODEL), f32),
        grid=(T // tm,),
        in_specs=[pl.BlockSpec((tm, D_MODEL), lambda i: (i, 0)),
                  pl.BlockSpec((tm, D_MODEL), lambda i: (i, 0)),
                  pl.BlockSpec((D_MODEL, D_MODEL), lambda i: (0, 0), pipeline_mode=pl.Buffered(1)),
                  pl.BlockSpec((1, D_MODEL), lambda i: (0, 0)),
                  pl.BlockSpec((1, D_MODEL), lambda i: (0, 0))],
        out_specs=pl.BlockSpec((tm, D_MODEL), lambda i: (i, 0)),
        compiler_params=_cparams(("parallel",)),
        name="outproj_ln",
    )(merged, x2, w_out, g, b)


HALO = 8
FFN_TN = 512
FFN_SUB = 128
FFN_ROWS = 128


def _ffn_kernel(xm_ref, xp_ref, xn_ref, wg_ref, wv_ref, cg_ref, cv_ref, wdn_ref, g_ref, b_ref, o_ref,
                xh_ref, up_ref, h_ref, *, tm, tn, tiles_per_seq):
    i = pl.program_id(0)
    n = pl.program_id(1)

    @pl.when(n == 0)
    def _():
        pos = i % tiles_per_seq
        keep_prev = (pos != 0).astype(f32)
        keep_next = (pos != tiles_per_seq - 1).astype(f32)
        xh_ref[0:HALO, :] = (xp_ref[...] * keep_prev).astype(bf16)
        xh_ref[HALO:HALO + tm, :] = xm_ref[...].astype(bf16)
        xh_ref[HALO + tm:2 * HALO + tm, :] = (xn_ref[...] * keep_next).astype(bf16)
        o_ref[...] = DEEPNORM_ALPHA * xm_ref[...]

    taps = (cg_ref[...], cv_ref[...])
    planes = FFN_SUB // LANES

    def up_dot(c):
        cols = slice(c * FFN_SUB, (c + 1) * FFN_SUB)
        w = jnp.concatenate([wg_ref[:, cols], wv_ref[:, cols]], axis=1)
        up = jnp.dot(xh_ref[...], w, preferred_element_type=f32)
        for q in range(2 * planes):
            up_ref[c % 2, q] = up[:, q * LANES:(q + 1) * LANES]

    def act(c):
        for p in range(planes):
            c0 = c * FFN_SUB + p * LANES
            for r in range(tm // FFN_ROWS):
                r0 = r * FFN_ROWS
                conv = [sum(up_ref[c % 2, gv * planes + p, r0 + HALO - 1 + j:r0 + HALO - 1 + j + FFN_ROWS, :]
                            * taps[gv][j:j + 1, c0:c0 + LANES] for j in range(3)) for gv in range(2)]
                h_ref[r0:r0 + FFN_ROWS, c0:c0 + LANES] = (_silu(conv[0]) * conv[1]).astype(bf16)

    nsub = tn // FFN_SUB
    up_dot(0)
    for c in range(nsub):
        if c + 1 < nsub:
            up_dot(c + 1)
        act(c)
    o_ref[...] += jnp.dot(h_ref[...], wdn_ref[...], preferred_element_type=f32)

    @pl.when(n == pl.num_programs(1) - 1)
    def _():
        o_ref[...] = _layer_norm(o_ref[...], g_ref[...], b_ref[...])


def _ffn_ln(x1, w_up_p, conv_p, w_down_p, g, b, *, S, tm, tn):
    T = x1.shape[0]
    nf = D_FF_PAD // tn
    hpt = tm // HALO
    last_halo = T // HALO - 1
    return pl.pallas_call(
        functools.partial(_ffn_kernel, tm=tm, tn=tn, tiles_per_seq=S // tm),
        out_shape=jax.ShapeDtypeStruct((T, D_MODEL), f32),
        grid=(T // tm, nf),
        in_specs=[pl.BlockSpec((tm, D_MODEL), lambda i, n: (i, 0)),
                  pl.BlockSpec((HALO, D_MODEL), lambda i, n: (jnp.maximum(i * hpt - 1, 0), 0)),
                  pl.BlockSpec((HALO, D_MODEL), lambda i, n: (jnp.minimum((i + 1) * hpt, last_halo), 0)),
                  pl.BlockSpec((D_MODEL, tn), lambda i, n: (0, n)),
                  pl.BlockSpec((D_MODEL, tn), lambda i, n: (0, nf + n)),
                  pl.BlockSpec((3, tn), lambda i, n: (0, n)),
                  pl.BlockSpec((3, tn), lambda i, n: (0, nf + n)),
                  pl.BlockSpec((tn, D_MODEL), lambda i, n: (n, 0)),
                  pl.BlockSpec((1, D_MODEL), lambda i, n: (0, 0)),
                  pl.BlockSpec((1, D_MODEL), lambda i, n: (0, 0))],
        out_specs=pl.BlockSpec((tm, D_MODEL), lambda i, n: (i, 0)),
        scratch_shapes=[pltpu.VMEM((tm + 2 * HALO, D_MODEL), bf16),
                        pltpu.VMEM((2, 2 * FFN_SUB // LANES, tm + 2 * HALO, LANES), f32),
                        pltpu.VMEM((tm, tn), bf16)],
        compiler_params=_cparams(("parallel", "arbitrary")),
        name="ffn_ln",
    )(x1, x1, x1, w_up_p, w_up_p, conv_p, conv_p, w_down_p, g, b)


def _pick(n, prefs):
    for p in prefs:
        if n % p == 0:
            return p
    raise ValueError(f"no tile in {prefs} divides {n}")


def _trunk(x, mem, wts):
    B, S, _ = x.shape
    T = B * S
    assert S % KV_TILE == 0 and (S // SB) % GDN_UNROLL == 0
    x2 = x.reshape(T, D_MODEL)

    proj2, ab2 = _inproj(x2, wts["w_all"], wts["b_all"], wts["w_ab"], tm=_pick(T, (1024, 512, 256)), tn=1024)
    proj3 = proj2.reshape(B, S, PROJ_COLS)
    ab3 = ab2.reshape(B, S, LANES)

    y_gdn = _gdn(proj3, ab3, wts["gdn_conv"], wts["a_log"], wts["dt_bias"], wts["gdn_norm_w"])
    y_diff = _diff_attention(proj3, wts["bias_tiles"], wts["lam"], wts["diff_norm_w"], qb=_pick(S, (512, 256)))
    M = mem.shape[1]
    kv = _matmul(mem.reshape(B * M, D_MODEL), wts["w_mem_kv"], tm=_pick(B * M, (512, 256)))
    y_cross = _cross_attention(proj3, kv.reshape(B, M, 2 * CROSS_HEADS * CROSS_HEAD_DIM), qb=_pick(S, (1024, 512, 256)))

    tm = _pick(S, (512, 256))
    merged = _merge(y_gdn.reshape(T, -1), y_diff.reshape(T, -1), y_cross.reshape(T, -1), proj2,
                    wts["w_bg"], wts["w_bd"], wts["w_bc"], tm=tm)
    x1 = _outproj_ln(merged, x2, wts["w_out"], wts["ln1_g"], wts["ln1_b"], tm=tm)
    y = _ffn_ln(x1, wts["w_up_p"], wts["ffn_conv_p"], wts["w_down_p"], wts["ln2_g"], wts["ln2_b"],
                S=S, tm=tm, tn=FFN_TN)
    return y.reshape(B, S, D_MODEL)


def _prep_weights(rel_bias, w_in, gdn_conv, gdn_a_log, gdn_dt_bias, gdn_norm_w, diff_lambda, diff_norm_w,
                  w_mem_kv, w_gate, b_gate, w_branch_gdn, w_branch_diff, w_branch_cross, w_out,
                  ln1_g, ln1_b, w_up, ffn_conv, w_down, ln2_g, ln2_b):
    l = 0
    wi = w_in[l]
    a0 = 4 * GDN_WIDTH
    a1 = a0 + 4 * GDN_HEADS
    w_lin = jnp.concatenate([wi[:, :a0], wi[:, a1:]], axis=1)
    w_all = jnp.concatenate([w_lin, w_gate[l]], axis=1).astype(bf16)
    b_all = jnp.concatenate([jnp.zeros((PROJ_LIN,), f32), b_gate[l].astype(f32)])[None, :]
    w_ab = jnp.pad(wi[:, a0:a1], ((0, 0), (0, LANES - 4 * GDN_HEADS))).astype(bf16)

    padc = D_FF_PAD - D_FF

    def pad_halves(a):
        zeros = jnp.zeros((a.shape[0], padc), a.dtype)
        return jnp.concatenate([a[:, :D_FF], zeros, a[:, D_FF:], zeros], axis=1)

    def lane_vector(p):
        return jnp.pad(p.astype(f32).reshape(1, -1), ((0, 0), (0, LANES - 2 * GDN_HEADS)))

    bias_tiles, lam = _bias_tiles(rel_bias.astype(f32), diff_lambda[l].astype(f32))
    return dict(
        w_all=w_all, b_all=b_all, w_ab=w_ab,
        gdn_conv=gdn_conv[l].astype(f32), a_log=lane_vector(gdn_a_log[l]), dt_bias=lane_vector(gdn_dt_bias[l]),
        gdn_norm_w=gdn_norm_w[l].astype(f32)[None, :],
        bias_tiles=bias_tiles, lam=lam, diff_norm_w=diff_norm_w[l].astype(f32)[None, :],
        w_mem_kv=w_mem_kv[l].astype(bf16),
        w_bg=w_branch_gdn[l].astype(bf16), w_bd=w_branch_diff[l].astype(bf16), w_bc=w_branch_cross[l].astype(bf16),
        w_out=w_out[l].astype(bf16),
        ln1_g=ln1_g[l].astype(f32)[None, :], ln1_b=ln1_b[l].astype(f32)[None, :],
        w_up_p=pad_halves(w_up[l].astype(bf16)), ffn_conv_p=pad_halves(ffn_conv[l].astype(f32)),
        w_down_p=jnp.pad(w_down[l], ((0, padc), (0, 0))).astype(bf16),
        ln2_g=ln2_g[l].astype(f32)[None, :], ln2_b=ln2_b[l].astype(f32)[None, :],
    )


def kernel(x_prompt, x_sample, mem_prompt, mem_sample, rel_bias, w_in, gdn_conv, gdn_a_log, gdn_dt_bias, gdn_norm_w, diff_lambda, diff_norm_w, w_mem_kv, w_gate, b_gate, w_branch_gdn, w_branch_diff, w_branch_cross, w_out, ln1_g, ln1_b, w_up, ffn_conv, w_down, ln2_g, ln2_b):
    wts = _prep_weights(rel_bias, w_in, gdn_conv, gdn_a_log, gdn_dt_bias, gdn_norm_w, diff_lambda, diff_norm_w,
                        w_mem_kv, w_gate, b_gate, w_branch_gdn, w_branch_diff, w_branch_cross, w_out,
                        ln1_g, ln1_b, w_up, ffn_conv, w_down, ln2_g, ln2_b)
    return (_trunk(x_prompt, mem_prompt, wts), _trunk(x_sample, mem_sample, wts))
```
